```python
import math
import jax
import jax.numpy as jnp
from jax import lax
import numpy as np

D_MODEL = 1024
BATCH = 16
SEQ = 2048
DEPTH = 2

GRID_W = 64
CTX_LEN = 256
N_EVEN = (DEPTH + 1) // 2
N_ODD = DEPTH // 2
NORM_EPS = 1e-6

S5_WIDTH = D_MODEL // 2
S5_GROUP = 16
S5_GROUPS = S5_WIDTH // S5_GROUP
S5_STATE = 64
S5_DT_MIN = 1e-3
S5_DT_MAX = 1e-1

MLA_HEADS = 8
MLA_NOPE = 64
MLA_ROPE = 32
MLA_QK = MLA_NOPE + MLA_ROPE
MLA_V = 64
MLA_Q_RANK = D_MODEL // 4
MLA_KV_RANK = D_MODEL // 8
ROPE_BASE = 10000.0
Q_BLOCK = 128
EVEN_IN = S5_WIDTH + MLA_Q_RANK + MLA_KV_RANK + MLA_ROPE
EVEN_OUT = S5_WIDTH + MLA_HEADS * MLA_V

GLA_HEADS = 4
GLA_KEY = D_MODEL // 2
GLA_VAL = D_MODEL
GLA_DK = GLA_KEY // GLA_HEADS
GLA_DV = GLA_VAL // GLA_HEADS
GLA_GATE_RANK = 16
GLA_TAU = 16.0
GLA_CHUNK = 64
ODD_IN = 2 * GLA_KEY + 2 * GLA_VAL + 2 * GLA_GATE_RANK

FFN_DENSE = 2816
N_EXPERTS = 8
TOP_K = 2
FFN_EXPERT = 3584

kernel_name = 'hybrid_s5_mla_gla_moe_diffusion_block'


def rms_norm(x, g):
    xf = x.astype(jnp.float32)
    y = xf * lax.rsqrt(jnp.mean(xf * xf, axis=-1, keepdims=True) + NORM_EPS)
    return (y * g.astype(jnp.float32)).astype(x.dtype)


def modulate(x, shift, scale):
    return x * (1.0 + scale) + shift


def swiglu(x, w_gate, w_up, w_down):
    return (jax.nn.silu(x @ w_gate) * (x @ w_up)) @ w_down


def axial_rope_tables(n_rows, dtype):
    half = MLA_ROPE // 2
    inv = 1.0 / (ROPE_BASE ** (jnp.arange(0, half, 2, dtype=jnp.float32) / half))
    rows = jnp.repeat(jnp.arange(n_rows, dtype=jnp.float32), GRID_W)
    cols = jnp.tile(jnp.arange(GRID_W, dtype=jnp.float32), n_rows)
    ang_r = rows[:, None] * inv
    ang_c = cols[:, None] * inv
    return (jnp.cos(ang_r).astype(dtype), jnp.sin(ang_r).astype(dtype),
            jnp.cos(ang_c).astype(dtype), jnp.sin(ang_c).astype(dtype))


def _rotate_half(v, cos, sin):
    n = v.shape[-1] // 2
    v1, v2 = v[..., :n], v[..., n:]
    return jnp.concatenate([v1 * cos - v2 * sin, v2 * cos + v1 * sin], axis=-1)


def axial_rope(x, tab):
    cos_r, sin_r, cos_c, sin_c = tab
    half = MLA_ROPE // 2
    return jnp.concatenate([_rotate_half(x[..., :half], cos_r[:, None], sin_r[:, None]),
                            _rotate_half(x[..., half:], cos_c[:, None], sin_c[:, None])], axis=-1)


def block_attention(q, k, v):
    b, t, h, dq = q.shape
    nb = t // Q_BLOCK
    scale = dq ** -0.5
    qb = jnp.moveaxis(q.reshape(b, nb, Q_BLOCK, h, dq), 1, 0)

    def one(qblk):
        s = jnp.einsum('bqhd,bkhd->bhqk', qblk, k).astype(jnp.float32) * scale
        p = jax.nn.softmax(s, axis=-1).astype(v.dtype)
        return jnp.einsum('bhqk,bkhd->bqhd', p, v)

    o = lax.map(one, qb)
    return jnp.moveaxis(o, 0, 1).reshape(b, t, h * v.shape[-1])


def mla_queries(cq, q_norm, w_uq, tab):
    b, t = cq.shape[:2]
    q = (rms_norm(cq, q_norm) @ w_uq).reshape(b, t, MLA_HEADS, MLA_QK)
    if tab is None:
        return q
    return jnp.concatenate([q[..., :MLA_NOPE], axial_rope(q[..., MLA_NOPE:], tab)], axis=-1)


def mla_keys_values(ckv, k_rope, kv_norm, w_ukv, tab):
    b, t = ckv.shape[:2]
    kv = (rms_norm(ckv, kv_norm) @ w_ukv).reshape(b, t, MLA_HEADS, MLA_NOPE + MLA_V)
    kr = k_rope[:, :, None, :]
    if tab is not None:
        kr = axial_rope(kr, tab)
    k = jnp.concatenate([kv[..., :MLA_NOPE], jnp.broadcast_to(kr, (b, t, MLA_HEADS, MLA_ROPE))], axis=-1)
    return k, kv[..., MLA_NOPE:]


def s5_discretize(lam_re, lam_im, log_dt, b_re, b_im):
    dt = jnp.exp(log_dt)[:, None]
    mag = jnp.exp(lam_re * dt)
    abar_re = mag * jnp.cos(lam_im * dt)
    abar_im = mag * jnp.sin(lam_im * dt)
    den = lam_re * lam_re + lam_im * lam_im
    nr = abar_re - 1.0
    coef_re = (nr * lam_re + abar_im * lam_im) / den
    coef_im = (abar_im * lam_re - nr * lam_im) / den
    bbar_re = coef_re[..., None] * b_re - coef_im[..., None] * b_im
    bbar_im = coef_re[..., None] * b_im + coef_im[..., None] * b_re
    return abar_re, abar_im, bbar_re, bbar_im


def _complex_combine(e1, e2):
    a1r, a1i, b1r, b1i = e1
    a2r, a2i, b2r, b2i = e2
    return (a1r * a2r - a1i * a2i, a1r * a2i + a1i * a2r,
            a2r * b1r - a2i * b1i + b2r, a2r * b1i + a2i * b1r + b2i)


def s5_scan(u, abar_re, abar_im, bbar_re, bbar_im, s0_re, s0_im, reverse):
    if reverse:
        u = jnp.flip(u, axis=1)
    bu_re = jnp.einsum('btgh,gph->btgp', u, bbar_re)
    bu_im = jnp.einsum('btgh,gph->btgp', u, bbar_im)
    a_re = jnp.broadcast_to(abar_re, bu_re.shape)
    a_im = jnp.broadcast_to(abar_im, bu_im.shape)
    pw_re, pw_im, s_re, s_im = lax.associative_scan(_complex_combine, (a_re, a_im, bu_re, bu_im), axis=1)
    s_re, s_im = (s_re + pw_re * s0_re[:, None] - pw_im * s0_im[:, None],
                  s_im + pw_re * s0_im[:, None] + pw_im * s0_re[:, None])
    fin_re, fin_im = s_re[:, -1], s_im[:, -1]
    if reverse:
        s_re, s_im = jnp.flip(s_re, axis=1), jnp.flip(s_im, axis=1)
    return s_re, s_im, fin_re, fin_im


def s5_readout(s_re, s_im, c_re, c_im):
    y = jnp.einsum('btgp,ghp->btgh', s_re, c_re) - jnp.einsum('btgp,ghp->btgh', s_im, c_im)
    return y.reshape(y.shape[0], y.shape[1], S5_WIDTH)


def s5_mixer(u_lat, u_ctx, lam_re, lam_im, log_dt, b_re, b_im, c_re, c_im, d_skip, w_glu, b_glu, need_ctx):
    bsz = u_lat.shape[0]
    ul = u_lat.reshape(bsz, u_lat.shape[1], S5_GROUPS, S5_GROUP)
    uc = u_ctx.reshape(bsz, u_ctx.shape[1], S5_GROUPS, S5_GROUP)
    zero = jnp.zeros((bsz, S5_GROUPS, S5_STATE), u_lat.dtype)
    y_lat = u_lat * d_skip
    y_ctx = u_ctx * d_skip if need_ctx else None
    for direction in range(2):
        rev = direction == 1
        abar_re, abar_im, bbar_re, bbar_im = s5_discretize(lam_re[direction], lam_im[direction], log_dt[direction], b_re[direction], b_im[direction])
        sc_re, sc_im, fin_re, fin_im = s5_scan(uc, abar_re, abar_im, bbar_re, bbar_im, zero, zero, rev)
        sl_re, sl_im, _, _ = s5_scan(ul, abar_re, abar_im, bbar_re, bbar_im, fin_re, fin_im, rev)
        y_lat = y_lat + s5_readout(sl_re, sl_im, c_re[direction], c_im[direction])
        if need_ctx:
            y_ctx = y_ctx + s5_readout(sc_re, sc_im, c_re[direction], c_im[direction])

    def glu(y):
        a = jax.nn.gelu(y)
        return a * jax.nn.sigmoid(a @ w_glu + b_glu)

    return glu(y_lat), (glu(y_ctx) if need_ctx else None)


def even_mixer(x_lat, x_ctx, w_in, lam_re, lam_im, log_dt, b_re, b_im, c_re, c_im, d_skip, w_glu, b_glu,
               q_norm, w_uq, kv_norm, w_ukv, w_out, tab, need_ctx):
    z_l = x_lat @ w_in
    z_c = x_ctx @ w_in
    i0 = S5_WIDTH
    i1 = i0 + MLA_Q_RANK
    i2 = i1 + MLA_KV_RANK
    s5_l, s5_c = s5_mixer(z_l[..., :i0], z_c[..., :i0], lam_re, lam_im, log_dt, b_re, b_im, c_re, c_im,
                          d_skip, w_glu, b_glu, need_ctx)
    k_l, v_l = mla_keys_values(z_l[..., i1:i2], z_l[..., i2:], kv_norm, w_ukv, tab)
    k_c, v_c = mla_keys_values(z_c[..., i1:i2], z_c[..., i2:], kv_norm, w_ukv, None)
    q_l = mla_queries(z_l[..., i0:i1], q_norm, w_uq, tab)
    a_l = block_attention(q_l, jnp.concatenate([k_c, k_l], axis=1), jnp.concatenate([v_c, v_l], axis=1))
    out_l = jnp.concatenate([s5_l, a_l], axis=-1) @ w_out
    out_c = None
    if need_ctx:
        q_c = mla_queries(z_c[..., i0:i1], q_norm, w_uq, None)
        a_c = block_attention(q_c, k_c, v_c)
        out_c = jnp.concatenate([s5_c, a_c], axis=-1) @ w_out
    return out_l, out_c


def gla_chunk_scan(q, k, v, log_a, s0):
    b, t, h, _ = q.shape
    dv = v.shape[-1]
    n = t // GLA_CHUNK

    def chunks(a):
        return jnp.moveaxis(a.reshape(b, n, GLA_CHUNK, h, a.shape[-1]), 1, 0)

    order = jnp.tril(jnp.ones((GLA_CHUNK, GLA_CHUNK), dtype=bool))[None, :, :, None, None]

    def step(state, inp):
        qc, kc, vc, ac = inp
        qc = qc.astype(jnp.float32)
        kc = kc.astype(jnp.float32)
        vc = vc.astype(jnp.float32)
        cum = jnp.cumsum(ac.astype(jnp.float32), axis=1)
        last = cum[:, -1]
        o_inter = jnp.einsum('blhk,bhkv->blhv', qc * jnp.exp(cum), state)
        decay = jnp.exp(jnp.where(order, cum[:, :, None] - cum[:, None], -jnp.inf))
        scores = jnp.einsum('blhk,bmhk,blmhk->bhlm', qc, kc, decay)
        o_intra = jnp.einsum('bhlm,bmhv->blhv', scores, vc)
        new_state = jnp.exp(last)[..., None] * state + jnp.einsum('bmhk,bmhv->bhkv', kc * jnp.exp(last[:, None] - cum), vc)
        return new_state, o_inter + o_intra

    s_fin, o = lax.scan(step, s0.astype(jnp.float32), (chunks(q), chunks(k), chunks(v), chunks(log_a)))
    o = jnp.moveaxis(o, 0, 1).reshape(b, t, h, dv)
    return o.astype(v.dtype), s_fin


def gla_direction(q_c, k_c, v_c, la_c, q_l, k_l, v_l, la_l, reverse):
    def flip(a):
        return jnp.flip(a, axis=1) if reverse else a
    zero = jnp.zeros((q_c.shape[0], GLA_HEADS, GLA_DK, GLA_DV), jnp.float32)
    o_c, s_c = gla_chunk_scan(flip(q_c), flip(k_c), flip(v_c), flip(la_c), zero)
    o_l, _ = gla_chunk_scan(flip(q_l), flip(k_l), flip(v_l), flip(la_l), s_c)
    return flip(o_l), flip(o_c)


def odd_mixer(x_lat, x_ctx, w_in, w_gate2, b_gate2, head_norm, w_out, need_ctx):
    i1 = GLA_KEY
    i2 = 2 * GLA_KEY
    i3 = i2 + GLA_VAL
    i4 = i3 + GLA_VAL
    i5 = i4 + GLA_GATE_RANK

    def project(x):
        z = x @ w_in
        b, t = z.shape[:2]
        q = z[..., :i1].reshape(b, t, GLA_HEADS, GLA_DK) * (GLA_DK ** -0.5)
        k = z[..., i1:i2].reshape(b, t, GLA_HEADS, GLA_DK)
        v = z[..., i2:i3].reshape(b, t, GLA_HEADS, GLA_DV)
        r = z[..., i3:i4]
        la_f = (jax.nn.log_sigmoid((z[..., i4:i5] @ w_gate2[0] + b_gate2[0]).astype(jnp.float32)) / GLA_TAU).reshape(b, t, GLA_HEADS, GLA_DK)
        la_b = (jax.nn.log_sigmoid((z[..., i5:] @ w_gate2[1] + b_gate2[1]).astype(jnp.float32)) / GLA_TAU).reshape(b, t, GLA_HEADS, GLA_DK)
        return q, k, v, r, la_f, la_b

    q_l, k_l, v_l, r_l, laf_l, lab_l = project(x_lat)
    q_c, k_c, v_c, r_c, laf_c, lab_c = project(x_ctx)
    of_l, of_c = gla_direction(q_c, k_c, v_c, laf_c, q_l, k_l, v_l, laf_l, False)
    ob_l, ob_c = gla_direction(q_c, k_c, v_c, lab_c, q_l, k_l, v_l, lab_l, True)

    def finish(o, r):
        b, t = o.shape[:2]
        o = rms_norm(o, head_norm).reshape(b, t, GLA_VAL) * jax.nn.silu(r)
        return o @ w_out

    out_l = finish(of_l + ob_l, r_l)
    out_c = finish(of_c + ob_c, r_c) if need_ctx else None
    return out_l, out_c


def moe_swiglu(x, router, w_gate, w_up, w_down):
    b, t, d = x.shape
    xf = x.reshape(b * t, d)
    logits = (xf @ router).astype(jnp.float32)
    top_val, top_idx = lax.top_k(logits, TOP_K)
    top_w = jax.nn.softmax(top_val, axis=-1)
    comb = jnp.sum(jax.nn.one_hot(top_idx, N_EXPERTS, dtype=jnp.float32) * top_w[..., None], axis=1).astype(x.dtype)
    y = jnp.zeros_like(xf)
    for e in range(N_EXPERTS):
        y = y + comb[:, e:e + 1] * swiglu(xf, w_gate[e], w_up[e], w_down[e])
    return y.reshape(b, t, d)


def setup_inputs(seed: int = 0) -> dict:
    key = jax.random.key(seed)
    ks = list(jax.random.split(key, 48))

    def nrm(shape, scale):
        return jax.random.normal(ks.pop(), shape, jnp.float32) * scale

    D = D_MODEL
    G, P, H = S5_GROUPS, S5_STATE, S5_GROUP
    n_idx = jnp.arange(P, dtype=jnp.float32)
    return {
        'x': nrm((BATCH, SEQ, D), 1.0),
        'c': nrm((BATCH, D), 1.0),
        'ctx': nrm((BATCH, CTX_LEN, D), 1.0),
        'c_ctx': nrm((D,), 1.0),
        'mod_w': nrm((DEPTH, D, 6 * D), D ** -0.5),
        'mod_b': nrm((DEPTH, 6 * D), 0.02),
        'norm_g': 1.0 + nrm((DEPTH, 4, D), 0.05),
        'ev_w_in': nrm((N_EVEN, D, EVEN_IN), D ** -0.5),
        's5_lam_re': -0.5 + nrm((N_EVEN, 2, G, P), 0.01),
        's5_lam_im': jnp.pi * n_idx + nrm((N_EVEN, 2, G, P), 0.01),
        's5_log_dt': jax.random.uniform(ks.pop(), (N_EVEN, 2, G), jnp.float32, math.log(S5_DT_MIN), math.log(S5_DT_MAX)),
        's5_b_re': nrm((N_EVEN, 2, G, P, H), (2 * H) ** -0.5),
        's5_b_im': nrm((N_EVEN, 2, G, P, H), (2 * H) ** -0.5),
        's5_c_re': nrm((N_EVEN, 2, G, H, P), (2 * P) ** -0.5),
        's5_c_im': nrm((N_EVEN, 2, G, H, P), (2 * P) ** -0.5),
        's5_d': nrm((N_EVEN, S5_WIDTH), 1.0),
        's5_w_glu': nrm((N_EVEN, S5_WIDTH, S5_WIDTH), S5_WIDTH ** -0.5),
        's5_b_glu': nrm((N_EVEN, S5_WIDTH), 0.02),
        'mla_q_norm': 1.0 + nrm((N_EVEN, MLA_Q_RANK), 0.05),
        'mla_w_uq': nrm((N_EVEN, MLA_Q_RANK, MLA_HEADS * MLA_QK), MLA_Q_RANK ** -0.5),
        'mla_kv_norm': 1.0 + nrm((N_EVEN, MLA_KV_RANK), 0.05),
        'mla_w_ukv': nrm((N_EVEN, MLA_KV_RANK, MLA_HEADS * (MLA_NOPE + MLA_V)), MLA_KV_RANK ** -0.5),
        'ev_w_out': nrm((N_EVEN, EVEN_OUT, D), EVEN_OUT ** -0.5),
        'ffn_w_gate': nrm((N_EVEN, D, FFN_DENSE), D ** -0.5),
        'ffn_w_up': nrm((N_EVEN, D, FFN_DENSE), D ** -0.5),
        'ffn_w_down': nrm((N_EVEN, FFN_DENSE, D), FFN_DENSE ** -0.5),
        'od_w_in': nrm((N_ODD, D, ODD_IN), D ** -0.5),
        'gla_w_gate2': nrm((N_ODD, 2, GLA_GATE_RANK, GLA_KEY), GLA_GATE_RANK ** -0.5),
        'gla_b_gate2': nrm((N_ODD, 2, GLA_KEY), 0.1),
        'gla_head_norm': 1.0 + nrm((N_ODD, GLA_DV), 0.05),
        'od_w_out': nrm((N_ODD, GLA_VAL, D), GLA_VAL ** -0.5),
        'moe_router': nrm((N_ODD, D, N_EXPERTS), D ** -0.5),
        'moe_w_gate': nrm((N_ODD, N_EXPERTS, D, FFN_EXPERT), D ** -0.5),
        'moe_w_up': nrm((N_ODD, N_EXPERTS, D, FFN_EXPERT), D ** -0.5),
        'moe_w_down': nrm((N_ODD, N_EXPERTS, FFN_EXPERT, D), FFN_EXPERT ** -0.5),
    }


def reference(x, c, ctx, c_ctx, mod_w, mod_b, norm_g, ev_w_in, s5_lam_re, s5_lam_im, s5_log_dt,
              s5_b_re, s5_b_im, s5_c_re, s5_c_im, s5_d, s5_w_glu, s5_b_glu, mla_q_norm, mla_w_uq,
              mla_kv_norm, mla_w_ukv, ev_w_out, ffn_w_gate, ffn_w_up, ffn_w_down, od_w_in,
              gla_w_gate2, gla_b_gate2, gla_head_norm, od_w_out, moe_router, moe_w_gate, moe_w_up,
              moe_w_down):
    n_rows = x.shape[1] // GRID_W
    tab = axial_rope_tables(n_rows, x.dtype)
    h, hc = x, ctx
    act_c = jax.nn.silu(c)
    act_ctx = jax.nn.silu(c_ctx)
    for i in range(DEPTH):
        need_ctx = i < DEPTH - 1
        j = i // 2
        mod_l = act_c @ mod_w[i] + mod_b[i]
        mod_c = act_ctx @ mod_w[i] + mod_b[i]
        sh1, sc1, g1, sh2, sc2, g2 = jnp.split(mod_l[:, None, :], 6, axis=-1)
        csh1, csc1, cg1, csh2, csc2, cg2 = jnp.split(mod_c, 6, axis=-1)
        xl = modulate(rms_norm(h, norm_g[i, 0]), sh1, sc1)
        xc = modulate(rms_norm(hc, norm_g[i, 0]), csh1, csc1)
        if i % 2 == 0:
            yl, yc = even_mixer(xl, xc, ev_w_in[j], s5_lam_re[j], s5_lam_im[j], s5_log_dt[j], s5_b_re[j],
                                s5_b_im[j], s5_c_re[j], s5_c_im[j], s5_d[j], s5_w_glu[j], s5_b_glu[j],
                                mla_q_norm[j], mla_w_uq[j], mla_kv_norm[j], mla_w_ukv[j], ev_w_out[j],
                                tab, need_ctx)
        else:
            yl, yc = odd_mixer(xl, xc, od_w_in[j], gla_w_gate2[j], gla_b_gate2[j], gla_head_norm[j],
                               od_w_out[j], need_ctx)
        h = h + g1 * rms_norm(yl, norm_g[i, 1])
        if need_ctx:
            hc = hc + cg1 * rms_norm(yc, norm_g[i, 1])
        xl = modulate(rms_norm(h, norm_g[i, 2]), sh2, sc2)
        if i % 2 == 0:
            fl = swiglu(xl, ffn_w_gate[j], ffn_w_up[j], ffn_w_down[j])
        else:
            fl = moe_swiglu(xl, moe_router[j], moe_w_gate[j], moe_w_up[j], moe_w_down[j])
        h = h + g2 * rms_norm(fl, norm_g[i, 3])
        if need_ctx:
            xc = modulate(rms_norm(hc, norm_g[i, 2]), csh2, csc2)
            if i % 2 == 0:
                fc = swiglu(xc, ffn_w_gate[j], ffn_w_up[j], ffn_w_down[j])
            else:
                fc = moe_swiglu(xc, moe_router[j], moe_w_gate[j], moe_w_up[j], moe_w_down[j])
            hc = hc + cg2 * rms_norm(fc, norm_g[i, 3])
    return h
```

```python
import functools
import math

import jax
import jax.numpy as jnp
from jax import lax
from jax.experimental import pallas as pl
from jax.experimental.pallas import tpu as pltpu

F32 = jnp.float32
BF16 = jnp.bfloat16

NORM_EPS = 1e-6
GRID_W = 64
S5_GROUP = 16
S5_STATE = 64
MLA_HEADS = 8
MLA_NOPE = 64
MLA_ROPE = 32
MLA_V = 64
ROPE_BASE = 10000.0
GLA_HEADS = 4
GLA_GATE_RANK = 16
GLA_TAU = 16.0
N_EXPERTS = 8

LANES = 128
TM = 256
S5_TT = 32
S5_CB = 512
MOE_RB = 128
MOE_FC = 512
VMEM_BIG = 56 * 1024 * 1024


def _dot(a, b):
    return jnp.dot(a, b, preferred_element_type=F32)


def _dot_nt(a, b):
    return lax.dot_general(a, b, (((1,), (1,)), ((), ())), preferred_element_type=F32)


def _dot_tn(a, b):
    return lax.dot_general(a, b, (((0,), (0,)), ((), ())), preferred_element_type=F32)


def _rms(x, g):
    return x * lax.rsqrt(jnp.mean(x * x, axis=-1, keepdims=True) + NORM_EPS) * g


def _silu(x):
    return x * jax.nn.sigmoid(x)


def _gelu_tanh(x):
    return 0.5 * x * (1.0 + jnp.tanh(math.sqrt(2.0 / math.pi) * (x + 0.044715 * (x * x * x))))


def _params(sem, vmem=None):
    return pltpu.CompilerParams(dimension_semantics=sem, vmem_limit_bytes=vmem)


def _const_spec(shape):
    nd = len(shape)
    return pl.BlockSpec(shape, lambda *_: (0,) * nd)


def _mod_kernel(c_ref, w_ref, b_ref, o_ref):
    a = _silu(c_ref[...]).astype(BF16)
    o_ref[0] = _dot(a, w_ref[0].astype(BF16)) + b_ref[0]


def _modulation(cc, mod_w, mod_b):
    depth, d, n = mod_w.shape
    rows = cc.shape[0]
    tn = n // 4
    return pl.pallas_call(
        _mod_kernel,
        out_shape=jax.ShapeDtypeStruct((depth, rows, n), F32),
        grid=(depth, n // tn),
        in_specs=[pl.BlockSpec((rows, d), lambda i, j: (0, 0)),
                  pl.BlockSpec((1, d, tn), lambda i, j: (i, 0, j)),
                  pl.BlockSpec((1, 1, tn), lambda i, j: (i, 0, j))],
        out_specs=pl.BlockSpec((1, rows, tn), lambda i, j: (i, 0, j)),
        compiler_params=_params(("arbitrary", "arbitrary"), VMEM_BIG),
        name="modulation",
    )(cc, mod_w, mod_b.reshape(depth, 1, n))


def _rope(x, tab_ref):
    return x * tab_ref[0] + pltpu.roll(x, LANES - 8, 1) * tab_ref[1] + pltpu.roll(x, 8, 1) * tab_ref[2]


def _even_in_kernel(ctx_ref, x_ref, mod_ref, g_ref, win_ref, qn_ref, wuq_ref, kvn_ref, wukv_ref,
                    tab_ref, u_ref, q_ref, k_ref, v_ref, *, q_scale):
    j = pl.program_id(1)
    xt = jnp.where(j == 0, ctx_ref[0], x_ref[0])
    m = mod_ref[0, 0]
    xn = _rms(xt, g_ref[...]) * (1.0 + m[1:2]) + m[0:1]
    z = _dot(xn.astype(BF16), win_ref[...])
    u_ref[...] = z[:, :512]
    cqn = _rms(z[:, 512:768], qn_ref[...]).astype(BF16)
    qall = _dot(cqn, wuq_ref[...])
    for h in range(MLA_HEADS):
        q_ref[0, h] = (_rope(qall[:, h * LANES:(h + 1) * LANES], tab_ref) * q_scale).astype(BF16)
    ckvn = _rms(z[:, 768:896], kvn_ref[...]).astype(BF16)
    kv = _dot(ckvn, wukv_ref[...])
    kr = _rope(z[:, 896:1024], tab_ref)
    for h in range(MLA_HEADS):
        k_ref[0, h] = (kv[:, h * LANES:(h + 1) * LANES] + kr).astype(BF16)
    v_ref[0] = kv[:, MLA_HEADS * LANES:].astype(BF16)


def _even_in(ctx, x, modv, g, w_in, q_norm, w_uq, kv_norm, w_ukv, tabs, q_scale):
    b, s, d = x.shape
    nt = s // TM + 1
    t_all = nt * TM
    return pl.pallas_call(
        functools.partial(_even_in_kernel, q_scale=q_scale),
        out_shape=(jax.ShapeDtypeStruct((t_all, b * 512), F32),
                   jax.ShapeDtypeStruct((b, MLA_HEADS, t_all, LANES), BF16),
                   jax.ShapeDtypeStruct((b, MLA_HEADS, t_all, LANES), BF16),
                   jax.ShapeDtypeStruct((b, t_all, MLA_HEADS * MLA_V), BF16)),
        grid=(b, nt),
        in_specs=[pl.BlockSpec((1, TM, d), lambda i, j: (i, 0, 0)),
                  pl.BlockSpec((1, TM, d), lambda i, j: (i, jnp.maximum(j - 1, 0), 0)),
                  pl.BlockSpec((1, 1, 6, d), lambda i, j: (i, jnp.minimum(j, 1), 0, 0)),
                  _const_spec((1, d)),
                  _const_spec(w_in.shape),
                  _const_spec(q_norm.shape),
                  _const_spec(w_uq.shape),
                  _const_spec(kv_norm.shape),
                  _const_spec(w_ukv.shape),
                  pl.BlockSpec((3, TM, LANES), lambda i, j: (0, j, 0))],
        out_specs=(pl.BlockSpec((TM, 512), lambda i, j: (j, i)),
                   pl.BlockSpec((1, MLA_HEADS, TM, LANES), lambda i, j: (i, 0, j, 0)),
                   pl.BlockSpec((1, MLA_HEADS, TM, LANES), lambda i, j: (i, 0, j, 0)),
                   pl.BlockSpec((1, TM, MLA_HEADS * MLA_V), lambda i, j: (i, j, 0))),
        compiler_params=_params(("arbitrary", "arbitrary"), VMEM_BIG),
        name="even_in",
    )(ctx, x, modv, g, w_in, q_norm, w_uq, kv_norm, w_ukv, tabs)


def _s5_scan_kernel(uf_ref, ur_ref, a_ref, bbd_ref, cbd_ref, yf_ref, yr_ref, buf_f, buf_r, st_ref,
                    *, tt, nb):
    @pl.when(pl.program_id(0) == 0)
    def _():
        st_ref[...] = jnp.zeros_like(st_ref)

    nblk = a_ref.shape[-1] // S5_CB
    for d, (u_ref, buf) in enumerate(((uf_ref, buf_f), (ur_ref, buf_r))):
        u = u_ref[...].astype(BF16)
        for c in range(nblk):
            bu = _dot(u[:, c * LANES:(c + 1) * LANES], bbd_ref[d, c])
            buf[0, :, c * S5_CB:(c + 1) * S5_CB] = bu[:, :S5_CB]
            buf[1, :, c * S5_CB:(c + 1) * S5_CB] = bu[:, S5_CB:]

    for c in range(nblk):
        sl = slice(c * S5_CB, (c + 1) * S5_CB)
        afr = jnp.broadcast_to(a_ref[0, 0:1, sl], (nb, S5_CB))
        afi = jnp.broadcast_to(a_ref[0, 1:2, sl], (nb, S5_CB))
        arr = jnp.broadcast_to(a_ref[1, 0:1, sl], (nb, S5_CB))
        ari = jnp.broadcast_to(a_ref[1, 1:2, sl], (nb, S5_CB))

        def body(t, carry, sl=sl, afr=afr, afi=afi, arr=arr, ari=ari):
            fr, fi, rr, ri = carry
            rf = pl.ds(pl.multiple_of(t * nb, nb), nb)
            rb = pl.ds(pl.multiple_of((tt - 1 - t) * nb, nb), nb)
            nfr = afr * fr - afi * fi + buf_f[0, rf, sl]
            nfi = afr * fi + afi * fr + buf_f[1, rf, sl]
            nrr = arr * rr - ari * ri + buf_r[0, rb, sl]
            nri = arr * ri + ari * rr + buf_r[1, rb, sl]
            buf_f[0, rf, sl] = nfr
            buf_f[1, rf, sl] = nfi
            buf_r[0, rb, sl] = nrr
            buf_r[1, rb, sl] = nri
            return nfr, nfi, nrr, nri

        init = (st_ref[0, 0, :, sl], st_ref[0, 1, :, sl], st_ref[1, 0, :, sl], st_ref[1, 1, :, sl])
        fr, fi, rr, ri = lax.fori_loop(0, tt, body, init, unroll=4)
        st_ref[0, 0, :, sl] = fr
        st_ref[0, 1, :, sl] = fi
        st_ref[1, 0, :, sl] = rr
        st_ref[1, 1, :, sl] = ri

    for d, (buf, y_ref) in enumerate(((buf_f, yf_ref), (buf_r, yr_ref))):
        for c in range(nblk):
            sl = slice(c * S5_CB, (c + 1) * S5_CB)
            y_ref[:, c * LANES:(c + 1) * LANES] = (
                _dot(buf[0, :, sl].astype(BF16), cbd_ref[d, c, :S5_CB])
                + _dot(buf[1, :, sl].astype(BF16), cbd_ref[d, c, S5_CB:]))


def _s5_scan(u_t, a, bbd, cbd, nb, n_ctx_tiles):
    rows, w = u_t.shape
    tt = S5_TT
    n = rows // (tt * nb)
    nc = n_ctx_tiles
    width = a.shape[-1]

    def fwd(j):
        return (j, 0)

    def rev(j):
        return (jnp.where(j < nc, nc - 1 - j, n - 1 - j + nc), 0)

    blk = (tt * nb, w)
    return pl.pallas_call(
        functools.partial(_s5_scan_kernel, tt=tt, nb=nb),
        out_shape=(jax.ShapeDtypeStruct((rows, w), F32), jax.ShapeDtypeStruct((rows, w), F32)),
        grid=(n,),
        in_specs=[pl.BlockSpec(blk, fwd), pl.BlockSpec(blk, rev),
                  _const_spec(a.shape), _const_spec(bbd.shape), _const_spec(cbd.shape)],
        out_specs=(pl.BlockSpec(blk, fwd), pl.BlockSpec(blk, rev)),
        scratch_shapes=[pltpu.VMEM((2, tt * nb, width), F32),
                        pltpu.VMEM((2, tt * nb, width), F32),
                        pltpu.VMEM((2, 2, nb, width), F32)],
        compiler_params=_params(("arbitrary",), VMEM_BIG),
        name="s5_scan",
    )(u_t, u_t, a, bbd, cbd)


def _attn_kernel(q_ref, k_ref, v_ref, o_ref, *, n_ctx):
    j = pl.program_id(2)
    lane = lax.broadcasted_iota(jnp.int32, (TM, LANES), 1)

    def run(nk):
        outs = []
        for h in range(2):
            s = _dot_nt(q_ref[0, h], k_ref[0, h, :nk])
            p = jnp.exp(s - jnp.max(s, axis=-1, keepdims=True))
            l = jnp.sum(p, axis=-1, keepdims=True)
            outs.append(_dot(p.astype(BF16), v_ref[0, :nk]) / l)
        o_ref[0] = jnp.where(lane < MLA_V, outs[0], outs[1]).astype(o_ref.dtype)

    @pl.when(j == 0)
    def _():
        run(n_ctx)

    @pl.when(j > 0)
    def _():
        run(k_ref.shape[2])


def _attention(q, k, v):
    b, h, t_all, _ = q.shape
    nt = t_all // TM
    return pl.pallas_call(
        functools.partial(_attn_kernel, n_ctx=TM),
        out_shape=jax.ShapeDtypeStruct((b, t_all, h * MLA_V), BF16),
        grid=(b, h // 2, nt),
        in_specs=[pl.BlockSpec((1, 2, TM, LANES), lambda i, p, j: (i, p, j, 0)),
                  pl.BlockSpec((1, 2, t_all, LANES), lambda i, p, j: (i, p, 0, 0)),
                  pl.BlockSpec((1, t_all, LANES), lambda i, p, j: (i, 0, p))],
        out_specs=pl.BlockSpec((1, TM, LANES), lambda i, p, j: (i, j, p)),
        compiler_params=_params(("arbitrary", "arbitrary", "arbitrary"), VMEM_BIG),
        name="mla_attention",
    )(q, k, v)


def _even_out_kernel(ctx_ref, x_ref, mod_ref, u_ref, yf_ref, yr_ref, a_ref, d_ref, wglu_ref, bglu_ref,
                     wout_ref, g_ref, h_ref):
    j = pl.program_id(1)
    h = jnp.where(j == 0, ctx_ref[0], x_ref[0])
    m = mod_ref[0, 0]
    u = u_ref[...]
    y = u * d_ref[...] + yf_ref[...] + yr_ref[...]
    act = _gelu_tanh(y)
    s5 = act * jax.nn.sigmoid(_dot(act.astype(BF16), wglu_ref[...]) + bglu_ref[...])
    mix = _dot(s5.astype(BF16), wout_ref[:512]) + _dot(a_ref[0], wout_ref[512:])
    h_ref[0] = h + m[2:3] * _rms(mix, g_ref[...])


def _even_out(ctx, x, modv, u_t, yf_t, yr_t, attn, d_skip, w_glu, b_glu, w_out, g):
    b, s, d = x.shape
    nt = s // TM + 1
    tok = pl.BlockSpec((TM, 512), lambda i, j: (j, i))
    return pl.pallas_call(
        _even_out_kernel,
        out_shape=jax.ShapeDtypeStruct((b, nt * TM, d), F32),
        grid=(b, nt),
        in_specs=[pl.BlockSpec((1, TM, d), lambda i, j: (i, 0, 0)),
                  pl.BlockSpec((1, TM, d), lambda i, j: (i, jnp.maximum(j - 1, 0), 0)),
                  pl.BlockSpec((1, 1, 6, d), lambda i, j: (i, jnp.minimum(j, 1), 0, 0)),
                  tok, tok, tok,
                  pl.BlockSpec((1, TM, 512), lambda i, j: (i, j, 0)),
                  _const_spec(d_skip.shape), _const_spec(w_glu.shape), _const_spec(b_glu.shape),
                  _const_spec(w_out.shape), _const_spec(g.shape)],
        out_specs=pl.BlockSpec((1, TM, d), lambda i, j: (i, j, 0)),
        compiler_params=_params(("arbitrary", "arbitrary"), VMEM_BIG),
        name="even_out",
    )(ctx, x, modv, u_t, yf_t, yr_t, attn, d_skip, w_glu, b_glu, w_out, g)


def _ffn_kernel(h_ref, mod_ref, g2_ref, g3_ref, wg_ref, wu_ref, wd_ref, o_ref):
    h = h_ref[0]
    m = mod_ref[0, 0]
    xn = (_rms(h, g2_ref[...]) * (1.0 + m[4:5]) + m[3:4]).astype(BF16)
    act = (_silu(_dot(xn, wg_ref[...])) * _dot(xn, wu_ref[...])).astype(BF16)
    f = _dot(act, wd_ref[...])
    o_ref[0] = h + m[5:6] * _rms(f, g3_ref[...])


def _ffn(h, modv, g2, g3, wg, wu, wd):
    b, t_all, d = h.shape
    nt = t_all // TM
    one = pl.Buffered(1)
    return pl.pallas_call(
        _ffn_kernel,
        out_shape=jax.ShapeDtypeStruct((b, t_all, d), F32),
        grid=(b, nt),
        in_specs=[pl.BlockSpec((1, TM, d), lambda i, j: (i, j, 0)),
                  pl.BlockSpec((1, 1, 6, d), lambda i, j: (i, jnp.minimum(j, 1), 0, 0)),
                  _const_spec(g2.shape), _const_spec(g3.shape),
                  pl.BlockSpec(wg.shape, lambda i, j: (0, 0), pipeline_mode=one),
                  pl.BlockSpec(wu.shape, lambda i, j: (0, 0), pipeline_mode=one),
                  pl.BlockSpec(wd.shape, lambda i, j: (0, 0), pipeline_mode=one)],
        out_specs=pl.BlockSpec((1, TM, d), lambda i, j: (i, j, 0)),
        compiler_params=_params(("arbitrary", "arbitrary"), VMEM_BIG),
        name="dense_ffn",
    )(h, modv, g2, g3, wg, wu, wd)


def _odd_in_kernel(h_ref, mod_ref, g_ref, win_ref, wg2_ref, bg2_ref, q_ref, k_ref, v_ref, r_ref,
                   laf_ref, lab_ref, *, q_scale):
    m = mod_ref[0, 0]
    xn = (_rms(h_ref[0], g_ref[...]) * (1.0 + m[1:2]) + m[0:1]).astype(BF16)
    z = _dot(xn, win_ref[...])
    q_ref[0] = (z[:, :512] * q_scale).astype(BF16)
    k_ref[0] = z[:, 512:1024].astype(BF16)
    v_ref[0] = z[:, 1024:2048].astype(BF16)
    r_ref[0] = z[:, 2048:3072].astype(BF16)
    gate = _dot(z[:, 3072:3200].astype(BF16), wg2_ref[...]) + bg2_ref[...]
    la = (jnp.minimum(gate, 0.0) - jnp.log1p(jnp.exp(-jnp.abs(gate)))) * (1.0 / GLA_TAU)
    laf_ref[0] = la[:, :512]
    lab_ref[0] = la[:, 512:]


def _odd_in(h, modv, g, w_in, w_gate2, b_gate2, q_scale):
    b, t_all, d = h.shape
    nt = t_all // TM

    def tok(w):
        return pl.BlockSpec((1, TM, w), lambda i, j: (i, j, 0))

    def out(w, dt):
        return jax.ShapeDtypeStruct((b, t_all, w), dt)

    return pl.pallas_call(
        functools.partial(_odd_in_kernel, q_scale=q_scale),
        out_shape=(out(512, BF16), out(512, BF16), out(1024, BF16), out(1024, BF16),
                   out(512, F32), out(512, F32)),
        grid=(b, nt),
        in_specs=[tok(d),
                  pl.BlockSpec((1, 1, 6, d), lambda i, j: (i, jnp.minimum(j, 1), 0, 0)),
                  _const_spec(g.shape), _const_spec(w_in.shape), _const_spec(w_gate2.shape),
                  _const_spec(b_gate2.shape)],
        out_specs=(tok(512), tok(512), tok(1024), tok(1024), tok(512), tok(512)),
        compiler_params=_params(("arbitrary", "arbitrary"), VMEM_BIG),
        name="odd_in",
    )(h, modv, g, w_in, w_gate2, b_gate2)


def _gla_kernel(qf_ref, kf_ref, vf_ref, laf_ref, qb_ref, kb_ref, vb_ref, lab_ref, of_ref, ob_ref,
                sf_ref, sb_ref):
    @pl.when(pl.program_id(1) == 0)
    def _():
        sf_ref[...] = jnp.zeros_like(sf_ref)
        sb_ref[...] = jnp.zeros_like(sb_ref)

    n = qf_ref.shape[1]
    dk = qf_ref.shape[2] // GLA_HEADS
    dv = vf_ref.shape[2] // GLA_HEADS
    row = lax.broadcasted_iota(jnp.int32, (n, n), 0)
    col = lax.broadcasted_iota(jnp.int32, (n, n), 1)
    mid = n // 2
    dirs = ((qf_ref, kf_ref, vf_ref, laf_ref, of_ref, sf_ref, col <= row, n - 1, mid - 1),
            (qb_ref, kb_ref, vb_ref, lab_ref, ob_ref, sb_ref, col >= row, 0, mid))
    for q_ref, k_ref, v_ref, la_ref, o_ref, s_ref, keep, i_tot, i_mid in dirs:
        la = la_ref[0]
        hi = la.astype(BF16)
        lo = (la - hi.astype(F32)).astype(BF16)
        tri = keep.astype(BF16)
        cum = _dot(tri, hi) + _dot(tri, lo)
        tot = cum[i_tot:i_tot + 1]
        cm = cum[i_mid:i_mid + 1]
        qe = q_ref[0].astype(F32) * jnp.exp(cum - cm)
        ke = k_ref[0].astype(F32) * jnp.exp(cm - cum)
        qi = (qe * jnp.exp(cm)).astype(BF16)
        k2 = (ke * jnp.exp(tot - cm)).astype(BF16)
        e_tot = jnp.exp(tot)
        qe = qe.astype(BF16)
        ke = ke.astype(BF16)
        for h in range(GLA_HEADS):
            ks = slice(h * dk, (h + 1) * dk)
            vs = slice(h * dv, (h + 1) * dv)
            v = v_ref[0, :, vs]
            sc = jnp.where(keep, _dot_nt(qe[:, ks], ke[:, ks]), 0.0).astype(BF16)
            st = s_ref[h]
            o_ref[0, :, vs] = _dot(sc, v) + _dot_nt(qi[:, ks], st.astype(BF16))
            s_ref[h] = st * e_tot[:, ks] + _dot_tn(v, k2[:, ks])


def _gla_scan(q, k, v, laf, lab):
    b, t_all, kw = q.shape
    vw = v.shape[2]
    n = t_all // TM

    def fwd(i, j):
        return (i, j, 0)

    def rev(i, j):
        return (i, jnp.where(j == 0, 0, n - j), 0)

    def spec(w, im):
        return pl.BlockSpec((1, TM, w), im)

    return pl.pallas_call(
        _gla_kernel,
        out_shape=(jax.ShapeDtypeStruct((b, t_all, vw), F32), jax.ShapeDtypeStruct((b, t_all, vw), F32)),
        grid=(b, n),
        in_specs=[spec(kw, fwd), spec(kw, fwd), spec(vw, fwd), spec(kw, fwd),
                  spec(kw, rev), spec(kw, rev), spec(vw, rev), spec(kw, rev)],
        out_specs=(spec(vw, fwd), spec(vw, rev)),
        scratch_shapes=[pltpu.VMEM((GLA_HEADS, vw // GLA_HEADS, kw // GLA_HEADS), F32),
                        pltpu.VMEM((GLA_HEADS, vw // GLA_HEADS, kw // GLA_HEADS), F32)],
        compiler_params=_params(("arbitrary", "arbitrary"), VMEM_BIG),
        name="gla_scan",
    )(q, k, v, laf, q, k, v, lab)


def _odd_out_kernel(h_ref, mod_ref, of_ref, ob_ref, r_ref, hn_ref, wout_ref, g1_ref, g2_ref, rt_ref,
                    h3_ref, xl_ref, comb_ref):
    m = mod_ref[0, 0]
    o = of_ref[0] + ob_ref[0]
    dv = hn_ref.shape[1]
    parts = [_rms(o[:, h * dv:(h + 1) * dv], hn_ref[...]) for h in range(GLA_HEADS)]
    on = jnp.concatenate(parts, axis=1) * _silu(r_ref[0].astype(F32))
    h3 = h_ref[0] + m[2:3] * _rms(_dot(on.astype(BF16), wout_ref[...]), g1_ref[...])
    h3_ref[0] = h3
    xl = _rms(h3, g2_ref[...]) * (1.0 + m[4:5]) + m[3:4]
    xl_ref[0] = xl.astype(BF16)
    xh = xl.astype(BF16)
    xo = (xl - xh.astype(F32)).astype(BF16)
    rt = rt_ref[...]
    rh = rt.astype(BF16)
    ro = (rt - rh.astype(F32)).astype(BF16)
    logit = _dot_nt(rh, xh) + (_dot_nt(rh, xo) + _dot_nt(ro, xh))
    ne = logit.shape[0]
    eid = lax.broadcasted_iota(jnp.int32, logit.shape, 0)
    m1 = jnp.max(logit, axis=0, keepdims=True)
    i1 = jnp.min(jnp.where(logit == m1, eid, ne), axis=0, keepdims=True)
    rest = jnp.where(eid == i1, -jnp.inf, logit)
    m2 = jnp.max(rest, axis=0, keepdims=True)
    i2 = jnp.min(jnp.where(rest == m2, eid, ne), axis=0, keepdims=True)
    e2 = jnp.exp(m2 - m1)
    w1 = 1.0 / (1.0 + e2)
    comb_ref[...] = jnp.where(eid == i1, w1, 0.0) + jnp.where(eid == i2, e2 * w1, 0.0)


def _odd_out(h, modv, of, ob, r, head_norm, w_out, g1, g2, router_t):
    b, t_all, d = h.shape
    nl = t_all // TM - 1

    def tok(w):
        return pl.BlockSpec((1, TM, w), lambda i, j: (i, j + 1, 0))

    def lat(w):
        return pl.BlockSpec((1, TM, w), lambda i, j: (i, j, 0))

    ne = router_t.shape[0]
    return pl.pallas_call(
        _odd_out_kernel,
        out_shape=(jax.ShapeDtypeStruct((b, nl * TM, d), F32),
                   jax.ShapeDtypeStruct((b, nl * TM, d), BF16),
                   jax.ShapeDtypeStruct((ne, b * nl * TM), F32)),
        grid=(b, nl),
        in_specs=[tok(d),
                  pl.BlockSpec((1, 1, 6, d), lambda i, j: (i, 1, 0, 0)),
                  tok(of.shape[2]), tok(ob.shape[2]), tok(r.shape[2]),
                  _const_spec(head_norm.shape), _const_spec(w_out.shape), _const_spec(g1.shape),
                  _const_spec(g2.shape), _const_spec(router_t.shape)],
        out_specs=(lat(d), lat(d), pl.BlockSpec((ne, TM), lambda i, j: (0, i * nl + j))),
        compiler_params=_params(("arbitrary", "arbitrary"), VMEM_BIG),
        name="odd_out",
    )(h, modv, of, ob, r, head_norm, w_out, g1, g2, router_t)


def _moe_kernel(x_ref, ct_ref, wg_ref, wu_ref, wd_ref, y_ref, xs_ref, acc_ref, tri_ref, cnt_ref,
                *, st, ns):
    b = pl.program_id(0)
    e = pl.program_id(1)
    c = pl.program_id(2)
    last_c = pl.num_programs(2) - 1
    rb = MOE_RB

    @pl.when((b == 0) & (e == 0) & (c == 0))
    def _():
        r = lax.broadcasted_iota(jnp.int32, (st, st), 0)
        cc = lax.broadcasted_iota(jnp.int32, (st, st), 1)
        tri_ref[...] = (r < cc).astype(BF16)

    @pl.when((e == 0) & (c == 0))
    def _():
        y_ref[...] = jnp.zeros_like(y_ref)
        xs_ref[...] = jnp.zeros_like(xs_ref)
        acc_ref[...] = jnp.zeros_like(acc_ref)

    def routed(s):
        w_row = ct_ref[pl.ds(e, 1), s * st:(s + 1) * st]
        sel = w_row > 0.0
        mask16 = jnp.broadcast_to(sel.astype(F32), (16, st)).astype(BF16)
        rank = _dot(mask16, tri_ref[...])[0:1]
        return w_row, sel, jnp.where(sel, rank, -1.0)

    def row_ids(i):
        return (lax.broadcasted_iota(jnp.int32, (rb, st), 0) + i * rb).astype(F32)

    def ceil_blocks(nrows):
        return lax.shift_right_logical(nrows + (rb - 1), int(math.log2(rb)))

    @pl.when(c == 0)
    def _gather():
        off = jnp.int32(0)
        total = jnp.int32(0)
        for s in range(ns):
            _, sel, rank = routed(s)
            cnt = jnp.sum(sel.astype(F32)).astype(jnp.int32)
            cnt_ref[s] = cnt

            def gather_block(i, carry, s=s, rank=rank, off=off):
                onehot = jnp.where(rank == row_ids(i), 1.0, 0.0).astype(BF16)
                rows = _dot(onehot, x_ref[0, s * st:(s + 1) * st, :])
                xs_ref[pl.ds(pl.multiple_of(off + i * rb, 16), rb), :] = rows.astype(BF16)
                return carry

            lax.fori_loop(0, ceil_blocks(cnt), gather_block, 0)
            total = off + cnt
            off = off + lax.shift_left(lax.shift_right_logical(cnt + 15, 4), 4)
        cnt_ref[ns] = total

        def zero_block(i, carry):
            acc_ref[pl.ds(pl.multiple_of(i * rb, rb), rb), :] = jnp.zeros((rb, acc_ref.shape[1]), F32)
            return carry

        lax.fori_loop(0, ceil_blocks(total), zero_block, 0)

    def ffn_block(i, carry):
        rows = pl.ds(pl.multiple_of(i * rb, rb), rb)
        xb = xs_ref[rows, :]
        act = (_silu(_dot(xb, wg_ref[0])) * _dot(xb, wu_ref[0])).astype(BF16)
        acc_ref[rows, :] += _dot(act, wd_ref[0])
        return carry

    lax.fori_loop(0, ceil_blocks(cnt_ref[ns]), ffn_block, 0)

    @pl.when(c == last_c)
    def _scatter():
        off = jnp.int32(0)
        for s in range(ns):
            w_row, _, rank = routed(s)
            cnt = cnt_ref[s]

            def scatter_block(i, carry, s=s, w_row=w_row, rank=rank, off=off):
                weighted = jnp.where(rank == row_ids(i), w_row, 0.0).astype(BF16)
                rows = acc_ref[pl.ds(pl.multiple_of(off + i * rb, 16), rb), :].astype(BF16)
                y_ref[0, s * st:(s + 1) * st, :] += _dot_tn(weighted, rows)
                return carry

            lax.fori_loop(0, ceil_blocks(cnt), scatter_block, 0)
            off = off + lax.shift_left(lax.shift_right_logical(cnt + 15, 4), 4)


def _moe(xl, comb_t, wg, wu, wd):
    b, t, d = xl.shape
    ne, _, f = wg.shape
    st = min(1024, t // 2)
    ns = t // st
    fc = MOE_FC
    return pl.pallas_call(
        functools.partial(_moe_kernel, st=st, ns=ns),
        out_shape=jax.ShapeDtypeStruct((b, t, d), F32),
        grid=(b, ne, f // fc),
        in_specs=[pl.BlockSpec((1, t, d), lambda i, e, c: (i, 0, 0)),
                  pl.BlockSpec((ne, t), lambda i, e, c: (0, i)),
                  pl.BlockSpec((1, d, fc), lambda i, e, c: (e, 0, c)),
                  pl.BlockSpec((1, d, fc), lambda i, e, c: (e, 0, c)),
                  pl.BlockSpec((1, fc, d), lambda i, e, c: (e, c, 0))],
        out_specs=pl.BlockSpec((1, t, d), lambda i, e, c: (i, 0, 0)),
        scratch_shapes=[pltpu.VMEM((t, d), BF16), pltpu.VMEM((t, d), F32),
                        pltpu.VMEM((st, st), BF16), pltpu.SMEM((ns + 1,), jnp.int32)],
        compiler_params=_params(("arbitrary", "arbitrary", "arbitrary"), VMEM_BIG),
        name="moe_ffn",
    )(xl, comb_t, wg, wu, wd)


def _final_kernel(h_ref, f_ref, mod_ref, g_ref, o_ref):
    m = mod_ref[0, 0]
    o_ref[0] = h_ref[0] + m[5:6] * _rms(f_ref[0], g_ref[...])


def _final(h3, fl, modv, g):
    b, t, d = h3.shape
    tok = pl.BlockSpec((1, TM, d), lambda i, j: (i, j, 0))
    return pl.pallas_call(
        _final_kernel,
        out_shape=jax.ShapeDtypeStruct((b, t, d), F32),
        grid=(b, t // TM),
        in_specs=[tok, tok, pl.BlockSpec((1, 1, 6, d), lambda i, j: (i, 1, 0, 0)), _const_spec(g.shape)],
        out_specs=tok,
        compiler_params=_params(("arbitrary", "arbitrary")),
        name="final_residual",
    )(h3, fl, modv, g)


def _rope_tables(n_rows, n_ctx):
    half = MLA_ROPE // 2
    inv = 1.0 / (ROPE_BASE ** (jnp.arange(0, half, 2, dtype=F32) / half))
    rows = jnp.repeat(jnp.arange(n_rows, dtype=F32), GRID_W)
    cols = jnp.tile(jnp.arange(GRID_W, dtype=F32), n_rows)
    ang_r = rows[:, None] * inv
    ang_c = cols[:, None] * inv
    cr, sr, cc, sc = jnp.cos(ang_r), jnp.sin(ang_r), jnp.cos(ang_c), jnp.sin(ang_c)
    t = rows.shape[0]
    one = jnp.ones((t, MLA_NOPE), F32)
    zero = jnp.zeros((t, MLA_NOPE), F32)
    z8 = jnp.zeros((t, half // 2), F32)
    pad1 = jnp.ones((t, LANES - MLA_NOPE - MLA_ROPE), F32)
    pad0 = jnp.zeros((t, LANES - MLA_NOPE - MLA_ROPE), F32)
    cos = jnp.concatenate([one, cr, cr, cc, cc, pad1], axis=1)
    s_up = jnp.concatenate([zero, -sr, z8, -sc, z8, pad0], axis=1)
    s_dn = jnp.concatenate([zero, z8, sr, z8, sc, pad0], axis=1)
    tab = jnp.stack([cos, s_up, s_dn])
    ctx = jnp.stack([jnp.ones((n_ctx, LANES), F32), jnp.zeros((n_ctx, LANES), F32),
                     jnp.zeros((n_ctx, LANES), F32)])
    return jnp.concatenate([ctx, tab], axis=1)


def _s5_discretize(lam_re, lam_im, log_dt, b_re, b_im):
    dt = jnp.exp(log_dt)[:, None]
    mag = jnp.exp(lam_re * dt)
    abar_re = mag * jnp.cos(lam_im * dt)
    abar_im = mag * jnp.sin(lam_im * dt)
    den = lam_re * lam_re + lam_im * lam_im
    nr = abar_re - 1.0
    coef_re = (nr * lam_re + abar_im * lam_im) / den
    coef_im = (abar_im * lam_re - nr * lam_im) / den
    bbar_re = coef_re[..., None] * b_re - coef_im[..., None] * b_im
    bbar_im = coef_re[..., None] * b_im + coef_im[..., None] * b_re
    return abar_re, abar_im, bbar_re, bbar_im


def _block_diag(m, per):
    g, r, c = m.shape
    eye = jnp.eye(per, dtype=m.dtype)
    m = m.reshape(g // per, per, r, c)
    return jnp.einsum("sarc,ab->sarbc", m, eye).reshape(g // per, per * r, per * c)


def _s5_layout(lam_re, lam_im, log_dt, b_re, b_im, c_re, c_im):
    per = LANES // S5_GROUP
    a, bbd, cbd = [], [], []
    for d in range(2):
        ar, ai, br, bi = _s5_discretize(lam_re[d], lam_im[d], log_dt[d], b_re[d], b_im[d])
        a.append(jnp.stack([ar.reshape(-1), ai.reshape(-1)]))
        bbd.append(jnp.concatenate([_block_diag(br.transpose(0, 2, 1), per),
                                    _block_diag(bi.transpose(0, 2, 1), per)], axis=2))
        cbd.append(jnp.concatenate([_block_diag(c_re[d].transpose(0, 2, 1), per),
                                    _block_diag(-c_im[d].transpose(0, 2, 1), per)], axis=1))
    return jnp.stack(a), jnp.stack(bbd).astype(BF16), jnp.stack(cbd).astype(BF16)


def _pad_heads(w, heads, width):
    k = w.shape[0]
    w = w.reshape(k, heads, -1)
    return jnp.pad(w, ((0, 0), (0, 0), (0, width - w.shape[2]))).reshape(k, heads * width)


def kernel(x, c, ctx, c_ctx, mod_w, mod_b, norm_g, ev_w_in, s5_lam_re, s5_lam_im, s5_log_dt, s5_b_re, s5_b_im, s5_c_re, s5_c_im, s5_d, s5_w_glu, s5_b_glu, mla_q_norm, mla_w_uq, mla_kv_norm, mla_w_ukv, ev_w_out, ffn_w_gate, ffn_w_up, ffn_w_down, od_w_in, gla_w_gate2, gla_b_gate2, gla_head_norm, od_w_out, moe_router, moe_w_gate, moe_w_up, moe_w_down):
    b, s, d = x.shape
    n_ctx = ctx.shape[1]
    assert n_ctx == TM and s % TM == 0 and mod_w.shape[0] == 2

    rows = -(-(b + 1) // 8) * 8
    cc = jnp.concatenate([c, c_ctx[None], jnp.zeros((rows - b - 1, d), F32)], axis=0)
    mods = _modulation(cc, mod_w, mod_b)

    def modv(i):
        lat = mods[i, :b].reshape(b, 1, 6, d)
        cx = jnp.broadcast_to(mods[i, b].reshape(1, 1, 6, d), (b, 1, 6, d))
        return jnp.concatenate([cx, lat], axis=1)

    def g(i, k):
        return norm_g[i, k].reshape(1, d)

    w_in = ev_w_in[0]
    kr_blk = jnp.pad(w_in[:, 896:928], ((0, 0), (MLA_NOPE, LANES - MLA_NOPE - MLA_ROPE)))
    w_in_p = jnp.concatenate([w_in[:, :896], kr_blk], axis=1).astype(BF16)
    w_uq_p = _pad_heads(mla_w_uq[0], MLA_HEADS, LANES).astype(BF16)
    ukv = mla_w_ukv[0].reshape(-1, MLA_HEADS, MLA_NOPE + MLA_V)
    w_ukv_p = jnp.concatenate(
        [jnp.pad(ukv[:, :, :MLA_NOPE], ((0, 0), (0, 0), (0, LANES - MLA_NOPE))).reshape(ukv.shape[0], -1),
         ukv[:, :, MLA_NOPE:].reshape(ukv.shape[0], -1)], axis=1).astype(BF16)
    tabs = _rope_tables(s // GRID_W, n_ctx)
    m0 = modv(0)
    u_t, q, k, v = _even_in(ctx, x, m0, g(0, 0), w_in_p, mla_q_norm[0].reshape(1, -1), w_uq_p,
                            mla_kv_norm[0].reshape(1, -1), w_ukv_p, tabs,
                            float((MLA_NOPE + MLA_ROPE) ** -0.5))
    a_s5, bbd, cbd = _s5_layout(s5_lam_re[0], s5_lam_im[0], s5_log_dt[0], s5_b_re[0], s5_b_im[0],
                                s5_c_re[0], s5_c_im[0])
    t_all = u_t.shape[0]
    yf, yr = _s5_scan(u_t.reshape(t_all * b, 512), a_s5, bbd, cbd, b, n_ctx // S5_TT)
    attn = _attention(q, k, v)
    h1 = _even_out(ctx, x, m0, u_t, yf.reshape(t_all, b * 512), yr.reshape(t_all, b * 512), attn,
                   s5_d[0].reshape(1, -1), s5_w_glu[0].astype(BF16), s5_b_glu[0].reshape(1, -1),
                   ev_w_out[0].astype(BF16), g(0, 1))
    h2 = _ffn(h1, m0, g(0, 2), g(0, 3), ffn_w_gate[0].astype(BF16), ffn_w_up[0].astype(BF16),
              ffn_w_down[0].astype(BF16))

    m1 = modv(1)
    gk = gla_w_gate2.shape[3]
    w_in1 = jnp.pad(od_w_in[0], ((0, 0), (0, LANES - 2 * GLA_GATE_RANK))).astype(BF16)
    wg2 = jnp.zeros((LANES, 2 * gk), F32)
    wg2 = wg2.at[:GLA_GATE_RANK, :gk].set(gla_w_gate2[0, 0])
    wg2 = wg2.at[GLA_GATE_RANK:2 * GLA_GATE_RANK, gk:].set(gla_w_gate2[0, 1]).astype(BF16)
    bg2 = gla_b_gate2[0].reshape(1, 2 * gk)
    ql, kl, vl, rl, laf, lab = _odd_in(h2, m1, g(1, 0), w_in1, wg2, bg2,
                                       float((gk // GLA_HEADS) ** -0.5))
    of, ob = _gla_scan(ql, kl, vl, laf, lab)
    h3, xl, comb_t = _odd_out(h2, m1, of, ob, rl, gla_head_norm[0].reshape(1, -1),
                              od_w_out[0].astype(BF16), g(1, 1), g(1, 2), moe_router[0].T)
    fl = _moe(xl, comb_t, moe_w_gate[0].astype(BF16), moe_w_up[0].astype(BF16),
              moe_w_down[0].astype(BF16))
    return _final(h3, fl, m1, g(1, 3))
```

```python
import functools
import math

import jax
import jax.numpy as jnp
from jax import lax
from jax.experimental import pallas as pl
from jax.experimental.pallas import tpu as pltpu

F32 = jnp.float32
BF16 = jnp.bfloat16

NORM_EPS = 1e-6
GRID_W = 64
S5_GROUP = 16
S5_STATE = 64
MLA_HEADS = 8
MLA_NOPE = 64
MLA_ROPE = 32
MLA_V = 64
ROPE_BASE = 10000.0
GLA_HEADS = 4
GLA_GATE_RANK = 16
GLA_TAU = 16.0
N_EXPERTS = 8

LANES = 128
TM = 256
S5_TT = 32
S5_CB = 512
MOE_GB = 288
MOE_FB = 288
MOE_SB = 256
MOE_FC = 512
VMEM_BIG = 56 * 1024 * 1024


def _dot(a, b):
    return jnp.dot(a, b, preferred_element_type=F32)


def _dot_nt(a, b):
    return lax.dot_general(a, b, (((1,), (1,)), ((), ())), preferred_element_type=F32)


def _dot_tn(a, b):
    return lax.dot_general(a, b, (((0,), (0,)), ((), ())), preferred_element_type=F32)


def _rms(x, g):
    return x * lax.rsqrt(jnp.mean(x * x, axis=-1, keepdims=True) + NORM_EPS) * g


def _silu(x):
    return x * jax.nn.sigmoid(x)


def _gelu_tanh(x):
    return 0.5 * x * (1.0 + jnp.tanh(math.sqrt(2.0 / math.pi) * (x + 0.044715 * (x * x * x))))


def _params(sem, vmem=None):
    return pltpu.CompilerParams(dimension_semantics=sem, vmem_limit_bytes=vmem)


def _const_spec(shape):
    nd = len(shape)
    return pl.BlockSpec(shape, lambda *_: (0,) * nd)


def _mod_kernel(c_ref, w_ref, b_ref, o_ref):
    a = _silu(c_ref[...]).astype(BF16)
    o_ref[0] = _dot(a, w_ref[0].astype(BF16)) + b_ref[0]


def _modulation(cc, mod_w, mod_b):
    depth, d, n = mod_w.shape
    rows = cc.shape[0]
    tn = n // 4
    return pl.pallas_call(
        _mod_kernel,
        out_shape=jax.ShapeDtypeStruct((depth, rows, n), F32),
        grid=(depth, n // tn),
        in_specs=[pl.BlockSpec((rows, d), lambda i, j: (0, 0)),
                  pl.BlockSpec((1, d, tn), lambda i, j: (i, 0, j)),
                  pl.BlockSpec((1, 1, tn), lambda i, j: (i, 0, j))],
        out_specs=pl.BlockSpec((1, rows, tn), lambda i, j: (i, 0, j)),
        compiler_params=_params(("arbitrary", "arbitrary"), VMEM_BIG),
        name="modulation",
    )(cc, mod_w, mod_b.reshape(depth, 1, n))


def _rope(x, tab_ref):
    return x * tab_ref[0] + pltpu.roll(x, LANES - 8, 1) * tab_ref[1] + pltpu.roll(x, 8, 1) * tab_ref[2]


def _even_in_kernel(ctx_ref, x_ref, mod_ref, g_ref, win_ref, qn_ref, wuq_ref, kvn_ref, wukv_ref,
                    tab_ref, u_ref, q_ref, k_ref, v_ref, *, q_scale):
    j = pl.program_id(1)
    xt = jnp.where(j == 0, ctx_ref[0], x_ref[0])
    m = mod_ref[0, 0]
    xn = _rms(xt, g_ref[...]) * (1.0 + m[1:2]) + m[0:1]
    z = _dot(xn.astype(BF16), win_ref[...])
    u_ref[...] = z[:, :512]
    cqn = _rms(z[:, 512:768], qn_ref[...]).astype(BF16)
    qall = _dot(cqn, wuq_ref[...])
    for h in range(MLA_HEADS):
        q_ref[0, h] = (_rope(qall[:, h * LANES:(h + 1) * LANES], tab_ref) * q_scale).astype(BF16)
    ckvn = _rms(z[:, 768:896], kvn_ref[...]).astype(BF16)
    kv = _dot(ckvn, wukv_ref[...])
    kr = _rope(z[:, 896:1024], tab_ref)
    for h in range(MLA_HEADS):
        k_ref[0, h] = (kv[:, h * LANES:(h + 1) * LANES] + kr).astype(BF16)
    v_ref[0] = kv[:, MLA_HEADS * LANES:].astype(BF16)


def _even_in(ctx, x, modv, g, w_in, q_norm, w_uq, kv_norm, w_ukv, tabs, q_scale):
    b, s, d = x.shape
    nt = s // TM + 1
    t_all = nt * TM
    return pl.pallas_call(
        functools.partial(_even_in_kernel, q_scale=q_scale),
        out_shape=(jax.ShapeDtypeStruct((t_all, b * 512), F32),
                   jax.ShapeDtypeStruct((b, MLA_HEADS, t_all, LANES), BF16),
                   jax.ShapeDtypeStruct((b, MLA_HEADS, t_all, LANES), BF16),
                   jax.ShapeDtypeStruct((b, t_all, MLA_HEADS * MLA_V), BF16)),
        grid=(b, nt),
        in_specs=[pl.BlockSpec((1, TM, d), lambda i, j: (i, 0, 0)),
                  pl.BlockSpec((1, TM, d), lambda i, j: (i, jnp.maximum(j - 1, 0), 0)),
                  pl.BlockSpec((1, 1, 6, d), lambda i, j: (i, jnp.minimum(j, 1), 0, 0)),
                  _const_spec((1, d)),
                  _const_spec(w_in.shape),
                  _const_spec(q_norm.shape),
                  _const_spec(w_uq.shape),
                  _const_spec(kv_norm.shape),
                  _const_spec(w_ukv.shape),
                  pl.BlockSpec((3, TM, LANES), lambda i, j: (0, j, 0))],
        out_specs=(pl.BlockSpec((TM, 512), lambda i, j: (j, i)),
                   pl.BlockSpec((1, MLA_HEADS, TM, LANES), lambda i, j: (i, 0, j, 0)),
                   pl.BlockSpec((1, MLA_HEADS, TM, LANES), lambda i, j: (i, 0, j, 0)),
                   pl.BlockSpec((1, TM, MLA_HEADS * MLA_V), lambda i, j: (i, j, 0))),
        compiler_params=_params(("arbitrary", "arbitrary"), VMEM_BIG),
        name="even_in",
    )(ctx, x, modv, g, w_in, q_norm, w_uq, kv_norm, w_ukv, tabs)


def _s5_scan_kernel(uf_ref, ur_ref, a_ref, bbd_ref, cbd_ref, yf_ref, yr_ref, buf_f, buf_r, st_ref,
                    *, tt, nb):
    @pl.when(pl.program_id(0) == 0)
    def _():
        st_ref[...] = jnp.zeros_like(st_ref)

    nblk = a_ref.shape[-1] // S5_CB
    for d, (u_ref, buf) in enumerate(((uf_ref, buf_f), (ur_ref, buf_r))):
        u = u_ref[...].astype(BF16)
        for c in range(nblk):
            bu = _dot(u[:, c * LANES:(c + 1) * LANES], bbd_ref[d, c])
            buf[0, :, c * S5_CB:(c + 1) * S5_CB] = bu[:, :S5_CB]
            buf[1, :, c * S5_CB:(c + 1) * S5_CB] = bu[:, S5_CB:]

    for c in range(nblk):
        sl = slice(c * S5_CB, (c + 1) * S5_CB)
        afr = jnp.broadcast_to(a_ref[0, 0:1, sl], (nb, S5_CB))
        afi = jnp.broadcast_to(a_ref[0, 1:2, sl], (nb, S5_CB))
        arr = jnp.broadcast_to(a_ref[1, 0:1, sl], (nb, S5_CB))
        ari = jnp.broadcast_to(a_ref[1, 1:2, sl], (nb, S5_CB))

        def body(t, carry, sl=sl, afr=afr, afi=afi, arr=arr, ari=ari):
            fr, fi, rr, ri = carry
            rf = pl.ds(pl.multiple_of(t * nb, nb), nb)
            rb = pl.ds(pl.multiple_of((tt - 1 - t) * nb, nb), nb)
            nfr = afr * fr - afi * fi + buf_f[0, rf, sl]
            nfi = afr * fi + afi * fr + buf_f[1, rf, sl]
            nrr = arr * rr - ari * ri + buf_r[0, rb, sl]
            nri = arr * ri + ari * rr + buf_r[1, rb, sl]
            buf_f[0, rf, sl] = nfr
            buf_f[1, rf, sl] = nfi
            buf_r[0, rb, sl] = nrr
            buf_r[1, rb, sl] = nri
            return nfr, nfi, nrr, nri

        init = (st_ref[0, 0, :, sl], st_ref[0, 1, :, sl], st_ref[1, 0, :, sl], st_ref[1, 1, :, sl])
        fr, fi, rr, ri = lax.fori_loop(0, tt, body, init, unroll=4)
        st_ref[0, 0, :, sl] = fr
        st_ref[0, 1, :, sl] = fi
        st_ref[1, 0, :, sl] = rr
        st_ref[1, 1, :, sl] = ri

    for d, (buf, y_ref) in enumerate(((buf_f, yf_ref), (buf_r, yr_ref))):
        for c in range(nblk):
            sl = slice(c * S5_CB, (c + 1) * S5_CB)
            y_ref[:, c * LANES:(c + 1) * LANES] = (
                _dot(buf[0, :, sl].astype(BF16), cbd_ref[d, c, :S5_CB])
                + _dot(buf[1, :, sl].astype(BF16), cbd_ref[d, c, S5_CB:])).astype(y_ref.dtype)


def _s5_scan(u_t, a, bbd, cbd, nb, n_ctx_tiles):
    rows, w = u_t.shape
    tt = S5_TT
    n = rows // (tt * nb)
    nc = n_ctx_tiles
    width = a.shape[-1]

    def fwd(j):
        return (j, 0)

    def rev(j):
        return (jnp.where(j < nc, nc - 1 - j, n - 1 - j + nc), 0)

    blk = (tt * nb, w)
    return pl.pallas_call(
        functools.partial(_s5_scan_kernel, tt=tt, nb=nb),
        out_shape=(jax.ShapeDtypeStruct((rows, w), BF16), jax.ShapeDtypeStruct((rows, w), BF16)),
        grid=(n,),
        in_specs=[pl.BlockSpec(blk, fwd), pl.BlockSpec(blk, rev),
                  _const_spec(a.shape), _const_spec(bbd.shape), _const_spec(cbd.shape)],
        out_specs=(pl.BlockSpec(blk, fwd), pl.BlockSpec(blk, rev)),
        scratch_shapes=[pltpu.VMEM((2, tt * nb, width), F32),
                        pltpu.VMEM((2, tt * nb, width), F32),
                        pltpu.VMEM((2, 2, nb, width), F32)],
        compiler_params=_params(("arbitrary",), VMEM_BIG),
        name="s5_scan",
    )(u_t, u_t, a, bbd, cbd)


def _attn_kernel(q_ref, k_ref, v_ref, o_ref, *, n_ctx):
    j = pl.program_id(1)
    lane = lax.broadcasted_iota(jnp.int32, (TM, LANES), 1)

    def run(nk):
        for hp in range(MLA_HEADS // 2):
            cols = slice(hp * LANES, (hp + 1) * LANES)
            outs = []
            for h in (2 * hp, 2 * hp + 1):
                s = _dot_nt(q_ref[0, h], k_ref[0, h, :nk])
                p = jnp.exp2(s - jnp.max(s, axis=-1, keepdims=True))
                l = jnp.sum(p, axis=-1, keepdims=True)
                outs.append(_dot(p.astype(BF16), v_ref[0, :nk, cols]) / l)
            o_ref[0, :, cols] = jnp.where(lane < MLA_V, outs[0], outs[1]).astype(o_ref.dtype)

    @pl.when(j == 0)
    def _():
        run(n_ctx)

    @pl.when(j > 0)
    def _():
        run(k_ref.shape[2])


def _attention(q, k, v):
    b, h, t_all, _ = q.shape
    nt = t_all // TM
    return pl.pallas_call(
        functools.partial(_attn_kernel, n_ctx=TM),
        out_shape=jax.ShapeDtypeStruct((b, t_all, h * MLA_V), BF16),
        grid=(b, nt),
        in_specs=[pl.BlockSpec((1, h, TM, LANES), lambda i, j: (i, 0, j, 0)),
                  pl.BlockSpec((1, h, t_all, LANES), lambda i, j: (i, 0, 0, 0)),
                  pl.BlockSpec((1, t_all, h * MLA_V), lambda i, j: (i, 0, 0))],
        out_specs=pl.BlockSpec((1, TM, h * MLA_V), lambda i, j: (i, j, 0)),
        compiler_params=_params(("arbitrary", "arbitrary"), VMEM_BIG),
        name="mla_attention",
    )(q, k, v)


def _even_out_kernel(ctx_ref, x_ref, mod_ref, u_ref, yf_ref, yr_ref, a_ref, d_ref, wglu_ref, bglu_ref,
                     wout_ref, g_ref, h_ref):
    j = pl.program_id(1)
    h = jnp.where(j == 0, ctx_ref[0], x_ref[0])
    m = mod_ref[0, 0]
    u = u_ref[...]
    y = u * d_ref[...] + yf_ref[...].astype(F32) + yr_ref[...].astype(F32)
    act = _gelu_tanh(y)
    s5 = act * jax.nn.sigmoid(_dot(act.astype(BF16), wglu_ref[...]) + bglu_ref[...])
    mix = _dot(s5.astype(BF16), wout_ref[:512]) + _dot(a_ref[0], wout_ref[512:])
    h_ref[0] = h + m[2:3] * _rms(mix, g_ref[...])


def _even_out(ctx, x, modv, u_t, yf_t, yr_t, attn, d_skip, w_glu, b_glu, w_out, g):
    b, s, d = x.shape
    nt = s // TM + 1
    tok = pl.BlockSpec((TM, 512), lambda i, j: (j, i))
    return pl.pallas_call(
        _even_out_kernel,
        out_shape=jax.ShapeDtypeStruct((b, nt * TM, d), F32),
        grid=(b, nt),
        in_specs=[pl.BlockSpec((1, TM, d), lambda i, j: (i, 0, 0)),
                  pl.BlockSpec((1, TM, d), lambda i, j: (i, jnp.maximum(j - 1, 0), 0)),
                  pl.BlockSpec((1, 1, 6, d), lambda i, j: (i, jnp.minimum(j, 1), 0, 0)),
                  tok, tok, tok,
                  pl.BlockSpec((1, TM, 512), lambda i, j: (i, j, 0)),
                  _const_spec(d_skip.shape), _const_spec(w_glu.shape), _const_spec(b_glu.shape),
                  _const_spec(w_out.shape), _const_spec(g.shape)],
        out_specs=pl.BlockSpec((1, TM, d), lambda i, j: (i, j, 0)),
        compiler_params=_params(("arbitrary", "arbitrary"), VMEM_BIG),
        name="even_out",
    )(ctx, x, modv, u_t, yf_t, yr_t, attn, d_skip, w_glu, b_glu, w_out, g)


def _ffn_kernel(h_ref, mod_ref, g2_ref, g3_ref, wg_ref, wu_ref, wd_ref, o_ref):
    h = h_ref[0]
    m = mod_ref[0, 0]
    xn = (_rms(h, g2_ref[...]) * (1.0 + m[4:5]) + m[3:4]).astype(BF16)
    act = (_silu(_dot(xn, wg_ref[...])) * _dot(xn, wu_ref[...])).astype(BF16)
    f = _dot(act, wd_ref[...])
    o_ref[0] = h + m[5:6] * _rms(f, g3_ref[...])


def _ffn(h, modv, g2, g3, wg, wu, wd):
    b, t_all, d = h.shape
    nt = t_all // TM
    one = pl.Buffered(1)
    return pl.pallas_call(
        _ffn_kernel,
        out_shape=jax.ShapeDtypeStruct((b, t_all, d), F32),
        grid=(b, nt),
        in_specs=[pl.BlockSpec((1, TM, d), lambda i, j: (i, j, 0)),
                  pl.BlockSpec((1, 1, 6, d), lambda i, j: (i, jnp.minimum(j, 1), 0, 0)),
                  _const_spec(g2.shape), _const_spec(g3.shape),
                  pl.BlockSpec(wg.shape, lambda i, j: (0, 0), pipeline_mode=one),
                  pl.BlockSpec(wu.shape, lambda i, j: (0, 0), pipeline_mode=one),
                  pl.BlockSpec(wd.shape, lambda i, j: (0, 0), pipeline_mode=one)],
        out_specs=pl.BlockSpec((1, TM, d), lambda i, j: (i, j, 0)),
        compiler_params=_params(("arbitrary", "arbitrary"), VMEM_BIG),
        name="dense_ffn",
    )(h, modv, g2, g3, wg, wu, wd)


def _odd_in_kernel(h_ref, mod_ref, g_ref, win_ref, wg2_ref, bg2_ref, q_ref, k_ref, v_ref, r_ref,
                   laf_ref, lab_ref, *, q_scale):
    m = mod_ref[0, 0]
    xn = (_rms(h_ref[0], g_ref[...]) * (1.0 + m[1:2]) + m[0:1]).astype(BF16)
    z = _dot(xn, win_ref[...])
    q_ref[0] = (z[:, :512] * q_scale).astype(BF16)
    k_ref[0] = z[:, 512:1024].astype(BF16)
    v_ref[0] = z[:, 1024:2048].astype(BF16)
    r_ref[0] = z[:, 2048:3072].astype(BF16)
    gate = _dot(z[:, 3072:3200].astype(BF16), wg2_ref[...]) + bg2_ref[...]
    la = (jnp.minimum(gate, 0.0) - jnp.log1p(jnp.exp(-jnp.abs(gate)))) * (1.0 / GLA_TAU)
    laf_ref[0] = la[:, :512]
    lab_ref[0] = la[:, 512:]


def _odd_in(h, modv, g, w_in, w_gate2, b_gate2, q_scale):
    b, t_all, d = h.shape
    nt = t_all // TM

    def tok(w):
        return pl.BlockSpec((1, TM, w), lambda i, j: (i, j, 0))

    def out(w, dt):
        return jax.ShapeDtypeStruct((b, t_all, w), dt)

    return pl.pallas_call(
        functools.partial(_odd_in_kernel, q_scale=q_scale),
        out_shape=(out(512, BF16), out(512, BF16), out(1024, BF16), out(1024, BF16),
                   out(512, F32), out(512, F32)),
        grid=(b, nt),
        in_specs=[tok(d),
                  pl.BlockSpec((1, 1, 6, d), lambda i, j: (i, jnp.minimum(j, 1), 0, 0)),
                  _const_spec(g.shape), _const_spec(w_in.shape), _const_spec(w_gate2.shape),
                  _const_spec(b_gate2.shape)],
        out_specs=(tok(512), tok(512), tok(1024), tok(1024), tok(512), tok(512)),
        compiler_params=_params(("arbitrary", "arbitrary"), VMEM_BIG),
        name="odd_in",
    )(h, modv, g, w_in, w_gate2, b_gate2)


def _gla_kernel(qf_ref, kf_ref, vf_ref, laf_ref, qb_ref, kb_ref, vb_ref, lab_ref, of_ref, ob_ref,
                sf_ref, sb_ref):
    @pl.when(pl.program_id(1) == 0)
    def _():
        sf_ref[...] = jnp.zeros_like(sf_ref)
        sb_ref[...] = jnp.zeros_like(sb_ref)

    n = qf_ref.shape[1]
    dk = qf_ref.shape[2] // GLA_HEADS
    dv = vf_ref.shape[2] // GLA_HEADS
    row = lax.broadcasted_iota(jnp.int32, (n, n), 0)
    col = lax.broadcasted_iota(jnp.int32, (n, n), 1)
    mid = n // 2
    dirs = ((qf_ref, kf_ref, vf_ref, laf_ref, of_ref, sf_ref, col <= row, n - 1, mid - 1),
            (qb_ref, kb_ref, vb_ref, lab_ref, ob_ref, sb_ref, col >= row, 0, mid))
    for q_ref, k_ref, v_ref, la_ref, o_ref, s_ref, keep, i_tot, i_mid in dirs:
        la = la_ref[0]
        hi = la.astype(BF16)
        lo = (la - hi.astype(F32)).astype(BF16)
        tri = keep.astype(BF16)
        cum = _dot(tri, hi) + _dot(tri, lo)
        tot = cum[i_tot:i_tot + 1]
        cm = cum[i_mid:i_mid + 1]
        qe = q_ref[0].astype(F32) * jnp.exp(cum - cm)
        ke = k_ref[0].astype(F32) * jnp.exp(cm - cum)
        qi = (qe * jnp.exp(cm)).astype(BF16)
        k2 = (ke * jnp.exp(tot - cm)).astype(BF16)
        e_tot = jnp.exp(tot)
        qe = qe.astype(BF16)
        ke = ke.astype(BF16)
        for h in range(GLA_HEADS):
            ks = slice(h * dk, (h + 1) * dk)
            vs = slice(h * dv, (h + 1) * dv)
            v = v_ref[0, :, vs]
            sc = jnp.where(keep, _dot_nt(qe[:, ks], ke[:, ks]), 0.0).astype(BF16)
            st = s_ref[h]
            o_ref[0, :, vs] = (_dot(sc, v) + _dot_nt(qi[:, ks], st.astype(BF16))).astype(o_ref.dtype)
            s_ref[h] = st * e_tot[:, ks] + _dot_tn(v, k2[:, ks])


def _gla_scan(q, k, v, laf, lab):
    b, t_all, kw = q.shape
    vw = v.shape[2]
    n = t_all // TM

    def fwd(i, j):
        return (i, j, 0)

    def rev(i, j):
        return (i, jnp.where(j == 0, 0, n - j), 0)

    def spec(w, im):
        return pl.BlockSpec((1, TM, w), im)

    return pl.pallas_call(
        _gla_kernel,
        out_shape=(jax.ShapeDtypeStruct((b, t_all, vw), BF16), jax.ShapeDtypeStruct((b, t_all, vw), BF16)),
        grid=(b, n),
        in_specs=[spec(kw, fwd), spec(kw, fwd), spec(vw, fwd), spec(kw, fwd),
                  spec(kw, rev), spec(kw, rev), spec(vw, rev), spec(kw, rev)],
        out_specs=(spec(vw, fwd), spec(vw, rev)),
        scratch_shapes=[pltpu.VMEM((GLA_HEADS, vw // GLA_HEADS, kw // GLA_HEADS), F32),
                        pltpu.VMEM((GLA_HEADS, vw // GLA_HEADS, kw // GLA_HEADS), F32)],
        compiler_params=_params(("arbitrary", "arbitrary"), VMEM_BIG),
        name="gla_scan",
    )(q, k, v, laf, q, k, v, lab)


def _odd_out_kernel(h_ref, mod_ref, of_ref, ob_ref, r_ref, hn_ref, wout_ref, g1_ref, g2_ref, rt_ref,
                    h3_ref, xl_ref, comb_ref):
    m = mod_ref[0, 0]
    o = of_ref[0].astype(F32) + ob_ref[0].astype(F32)
    dv = hn_ref.shape[1]
    parts = [_rms(o[:, h * dv:(h + 1) * dv], hn_ref[...]) for h in range(GLA_HEADS)]
    on = jnp.concatenate(parts, axis=1) * _silu(r_ref[0].astype(F32))
    h3 = h_ref[0] + m[2:3] * _rms(_dot(on.astype(BF16), wout_ref[...]), g1_ref[...])
    h3_ref[0] = h3
    xl = _rms(h3, g2_ref[...]) * (1.0 + m[4:5]) + m[3:4]
    xl_ref[0] = xl.astype(BF16)
    xh = xl.astype(BF16)
    xo = (xl - xh.astype(F32)).astype(BF16)
    rt = rt_ref[...]
    rh = rt.astype(BF16)
    ro = (rt - rh.astype(F32)).astype(BF16)
    logit = _dot_nt(rh, xh) + (_dot_nt(rh, xo) + _dot_nt(ro, xh))
    ne = logit.shape[0]
    eid = lax.broadcasted_iota(jnp.int32, logit.shape, 0)
    m1 = jnp.max(logit, axis=0, keepdims=True)
    i1 = jnp.min(jnp.where(logit == m1, eid, ne), axis=0, keepdims=True)
    rest = jnp.where(eid == i1, -jnp.inf, logit)
    m2 = jnp.max(rest, axis=0, keepdims=True)
    i2 = jnp.min(jnp.where(rest == m2, eid, ne), axis=0, keepdims=True)
    e2 = jnp.exp(m2 - m1)
    w1 = 1.0 / (1.0 + e2)
    comb_ref[...] = jnp.where(eid == i1, w1, 0.0) + jnp.where(eid == i2, e2 * w1, 0.0)


def _odd_out(h, modv, of, ob, r, head_norm, w_out, g1, g2, router_t):
    b, t_all, d = h.shape
    nl = t_all // TM - 1

    def tok(w):
        return pl.BlockSpec((1, TM, w), lambda i, j: (i, j + 1, 0))

    def lat(w):
        return pl.BlockSpec((1, TM, w), lambda i, j: (i, j, 0))

    ne = router_t.shape[0]
    return pl.pallas_call(
        _odd_out_kernel,
        out_shape=(jax.ShapeDtypeStruct((b, nl * TM, d), F32),
                   jax.ShapeDtypeStruct((b, nl * TM, d), BF16),
                   jax.ShapeDtypeStruct((ne, b * nl * TM), F32)),
        grid=(b, nl),
        in_specs=[tok(d),
                  pl.BlockSpec((1, 1, 6, d), lambda i, j: (i, 1, 0, 0)),
                  tok(of.shape[2]), tok(ob.shape[2]), tok(r.shape[2]),
                  _const_spec(head_norm.shape), _const_spec(w_out.shape), _const_spec(g1.shape),
                  _const_spec(g2.shape), _const_spec(router_t.shape)],
        out_specs=(lat(d), lat(d), pl.BlockSpec((ne, TM), lambda i, j: (0, i * nl + j))),
        compiler_params=_params(("arbitrary", "arbitrary"), VMEM_BIG),
        name="odd_out",
    )(h, modv, of, ob, r, head_norm, w_out, g1, g2, router_t)


def _ceil_count(n, step, cap):
    total = jnp.int32(0)
    for k in range(0, cap, step):
        total = total + (n > k).astype(jnp.int32)
    return total


def _round16(n):
    return lax.shift_left(lax.shift_right_logical(n + 15, 4), 4)


def _moe_kernel(x_ref, ct_ref, wg_ref, wu_ref, wd_ref, y_ref, xs_ref, acc_ref, tri_ref, rank_ref,
                cnt_ref, *, st, ns):
    b = pl.program_id(0)
    e = pl.program_id(1)
    c = pl.program_id(2)
    last_c = pl.num_programs(2) - 1
    ne = ct_ref.shape[0]
    cap = xs_ref.shape[0]
    trip = 2 * MOE_FB

    @pl.when((b == 0) & (e == 0) & (c == 0))
    def _():
        r = lax.broadcasted_iota(jnp.int32, (st, st), 0)
        cc = lax.broadcasted_iota(jnp.int32, (st, st), 1)
        tri_ref[...] = (r < cc).astype(BF16)

    @pl.when((e == 0) & (c == 0))
    def _():
        y_ref[...] = jnp.zeros_like(y_ref)
        xs_ref[...] = jnp.zeros_like(xs_ref)
        acc_ref[...] = jnp.zeros_like(acc_ref)
        for s in range(ns):
            sel = ct_ref[:, s * st:(s + 1) * st] > 0.0
            m16 = jnp.concatenate([sel.astype(F32), jnp.zeros((16 - ne, st), F32)], axis=0).astype(BF16)
            rank = _dot(m16, tri_ref[...])[:ne]
            rank_ref[:, s * st:(s + 1) * st] = jnp.where(sel, rank, -1.0)

    def routed(s):
        cols = slice(s * st, (s + 1) * st)
        return ct_ref[pl.ds(e, 1), cols], rank_ref[pl.ds(e, 1), cols]

    def row_ids(i, rows):
        return (lax.broadcasted_iota(jnp.int32, (rows, st), 0) + i * rows).astype(F32)

    @pl.when(c == 0)
    def _gather():
        off = jnp.int32(0)
        total = jnp.int32(0)
        for s in range(ns):
            w_row, rank = routed(s)
            cnt = jnp.sum((w_row > 0.0).astype(F32)).astype(jnp.int32)
            cnt_ref[s] = cnt

            def gather_block(i, carry, s=s, rank=rank, off=off):
                onehot = jnp.where(rank == row_ids(i, MOE_GB), 1.0, 0.0).astype(BF16)
                rows = _dot(onehot, x_ref[0, s * st:(s + 1) * st, :])
                xs_ref[pl.ds(pl.multiple_of(off + i * MOE_GB, 16), MOE_GB), :] = rows.astype(BF16)
                return carry

            lax.fori_loop(0, _ceil_count(cnt, MOE_GB, st), gather_block, 0)
            total = off + cnt
            off = off + _round16(cnt)
        cnt_ref[ns] = total

        def zero_trip(i, carry):
            acc_ref[pl.ds(pl.multiple_of(i * trip, 16), trip), :] = jnp.zeros((trip, acc_ref.shape[1]), F32)
            return carry

        lax.fori_loop(0, _ceil_count(total, trip, cap), zero_trip, 0)

    def ffn_trip(i, carry):
        for k in range(2):
            rows = pl.ds(pl.multiple_of(i * trip + k * MOE_FB, 16), MOE_FB)
            xb = xs_ref[rows, :]
            act = (_silu(_dot(xb, wg_ref[0])) * _dot(xb, wu_ref[0])).astype(BF16)
            acc_ref[rows, :] += _dot(act, wd_ref[0])
        return carry

    lax.fori_loop(0, _ceil_count(cnt_ref[ns], trip, cap), ffn_trip, 0)

    @pl.when(c == last_c)
    def _scatter():
        off = jnp.int32(0)
        for s in range(ns):
            w_row, rank = routed(s)
            cnt = cnt_ref[s]

            def scatter_block(i, carry, s=s, w_row=w_row, rank=rank, off=off):
                weighted = jnp.where(rank == row_ids(i, MOE_SB), w_row, 0.0).astype(BF16)
                rows = acc_ref[pl.ds(pl.multiple_of(off + i * MOE_SB, 16), MOE_SB), :].astype(BF16)
                y_ref[0, s * st:(s + 1) * st, :] += _dot_tn(weighted, rows)
                return carry

            lax.fori_loop(0, _ceil_count(cnt, MOE_SB, st), scatter_block, 0)
            off = off + _round16(cnt)


def _moe(xl, comb_t, wg, wu, wd):
    b, t, d = xl.shape
    ne, _, f = wg.shape
    st = min(1024, t // 2)
    ns = t // st
    fc = MOE_FC
    trip = 2 * MOE_FB
    need = max(t, (ns - 1) * st + -(-st // MOE_GB) * MOE_GB)
    cap = -(-need // trip) * trip
    return pl.pallas_call(
        functools.partial(_moe_kernel, st=st, ns=ns),
        out_shape=jax.ShapeDtypeStruct((b, t, d), F32),
        grid=(b, ne, f // fc),
        in_specs=[pl.BlockSpec((1, t, d), lambda i, e, c: (i, 0, 0)),
                  pl.BlockSpec((ne, t), lambda i, e, c: (0, i)),
                  pl.BlockSpec((1, d, fc), lambda i, e, c: (e, 0, c)),
                  pl.BlockSpec((1, d, fc), lambda i, e, c: (e, 0, c)),
                  pl.BlockSpec((1, fc, d), lambda i, e, c: (e, c, 0))],
        out_specs=pl.BlockSpec((1, t, d), lambda i, e, c: (i, 0, 0)),
        scratch_shapes=[pltpu.VMEM((cap, d), BF16), pltpu.VMEM((cap, d), F32),
                        pltpu.VMEM((st, st), BF16), pltpu.VMEM((ne, t), F32),
                        pltpu.SMEM((ns + 1,), jnp.int32)],
        compiler_params=_params(("arbitrary", "arbitrary", "arbitrary"), VMEM_BIG),
        name="moe_ffn",
    )(xl, comb_t, wg, wu, wd)


def _final_kernel(h_ref, f_ref, mod_ref, g_ref, o_ref):
    m = mod_ref[0, 0]
    o_ref[0] = h_ref[0] + m[5:6] * _rms(f_ref[0], g_ref[...])


def _final(h3, fl, modv, g):
    b, t, d = h3.shape
    tok = pl.BlockSpec((1, TM, d), lambda i, j: (i, j, 0))
    return pl.pallas_call(
        _final_kernel,
        out_shape=jax.ShapeDtypeStruct((b, t, d), F32),
        grid=(b, t // TM),
        in_specs=[tok, tok, pl.BlockSpec((1, 1, 6, d), lambda i, j: (i, 1, 0, 0)), _const_spec(g.shape)],
        out_specs=tok,
        compiler_params=_params(("arbitrary", "arbitrary")),
        name="final_residual",
    )(h3, fl, modv, g)


def _rope_tables(n_rows, n_ctx):
    half = MLA_ROPE // 2
    inv = 1.0 / (ROPE_BASE ** (jnp.arange(0, half, 2, dtype=F32) / half))
    rows = jnp.repeat(jnp.arange(n_rows, dtype=F32), GRID_W)
    cols = jnp.tile(jnp.arange(GRID_W, dtype=F32), n_rows)
    ang_r = rows[:, None] * inv
    ang_c = cols[:, None] * inv
    cr, sr, cc, sc = jnp.cos(ang_r), jnp.sin(ang_r), jnp.cos(ang_c), jnp.sin(ang_c)
    t = rows.shape[0]
    one = jnp.ones((t, MLA_NOPE), F32)
    zero = jnp.zeros((t, MLA_NOPE), F32)
    z8 = jnp.zeros((t, half // 2), F32)
    pad1 = jnp.ones((t, LANES - MLA_NOPE - MLA_ROPE), F32)
    pad0 = jnp.zeros((t, LANES - MLA_NOPE - MLA_ROPE), F32)
    cos = jnp.concatenate([one, cr, cr, cc, cc, pad1], axis=1)
    s_up = jnp.concatenate([zero, -sr, z8, -sc, z8, pad0], axis=1)
    s_dn = jnp.concatenate([zero, z8, sr, z8, sc, pad0], axis=1)
    tab = jnp.stack([cos, s_up, s_dn])
    ctx = jnp.stack([jnp.ones((n_ctx, LANES), F32), jnp.zeros((n_ctx, LANES), F32),
                     jnp.zeros((n_ctx, LANES), F32)])
    return jnp.concatenate([ctx, tab], axis=1)


def _s5_discretize(lam_re, lam_im, log_dt, b_re, b_im):
    dt = jnp.exp(log_dt)[:, None]
    mag = jnp.exp(lam_re * dt)
    abar_re = mag * jnp.cos(lam_im * dt)
    abar_im = mag * jnp.sin(lam_im * dt)
    den = lam_re * lam_re + lam_im * lam_im
    nr = abar_re - 1.0
    coef_re = (nr * lam_re + abar_im * lam_im) / den
    coef_im = (abar_im * lam_re - nr * lam_im) / den
    bbar_re = coef_re[..., None] * b_re - coef_im[..., None] * b_im
    bbar_im = coef_re[..., None] * b_im + coef_im[..., None] * b_re
    return abar_re, abar_im, bbar_re, bbar_im


def _block_diag(m, per):
    g, r, c = m.shape
    eye = jnp.eye(per, dtype=m.dtype)
    m = m.reshape(g // per, per, r, c)
    return jnp.einsum("sarc,ab->sarbc", m, eye).reshape(g // per, per * r, per * c)


def _s5_layout(lam_re, lam_im, log_dt, b_re, b_im, c_re, c_im):
    per = LANES // S5_GROUP
    a, bbd, cbd = [], [], []
    for d in range(2):
        ar, ai, br, bi = _s5_discretize(lam_re[d], lam_im[d], log_dt[d], b_re[d], b_im[d])
        a.append(jnp.stack([ar.reshape(-1), ai.reshape(-1)]))
        bbd.append(jnp.concatenate([_block_diag(br.transpose(0, 2, 1), per),
                                    _block_diag(bi.transpose(0, 2, 1), per)], axis=2))
        cbd.append(jnp.concatenate([_block_diag(c_re[d].transpose(0, 2, 1), per),
                                    _block_diag(-c_im[d].transpose(0, 2, 1), per)], axis=1))
    return jnp.stack(a), jnp.stack(bbd).astype(BF16), jnp.stack(cbd).astype(BF16)


def _pad_heads(w, heads, width):
    k = w.shape[0]
    w = w.reshape(k, heads, -1)
    return jnp.pad(w, ((0, 0), (0, 0), (0, width - w.shape[2]))).reshape(k, heads * width)


def kernel(x, c, ctx, c_ctx, mod_w, mod_b, norm_g, ev_w_in, s5_lam_re, s5_lam_im, s5_log_dt, s5_b_re, s5_b_im, s5_c_re, s5_c_im, s5_d, s5_w_glu, s5_b_glu, mla_q_norm, mla_w_uq, mla_kv_norm, mla_w_ukv, ev_w_out, ffn_w_gate, ffn_w_up, ffn_w_down, od_w_in, gla_w_gate2, gla_b_gate2, gla_head_norm, od_w_out, moe_router, moe_w_gate, moe_w_up, moe_w_down):
    b, s, d = x.shape
    n_ctx = ctx.shape[1]
    assert n_ctx == TM and s % TM == 0 and mod_w.shape[0] == 2

    rows = -(-(b + 1) // 8) * 8
    cc = jnp.concatenate([c, c_ctx[None], jnp.zeros((rows - b - 1, d), F32)], axis=0)
    mods = _modulation(cc, mod_w, mod_b)

    def modv(i):
        lat = mods[i, :b].reshape(b, 1, 6, d)
        cx = jnp.broadcast_to(mods[i, b].reshape(1, 1, 6, d), (b, 1, 6, d))
        return jnp.concatenate([cx, lat], axis=1)

    def g(i, k):
        return norm_g[i, k].reshape(1, d)

    w_in = ev_w_in[0]
    kr_blk = jnp.pad(w_in[:, 896:928], ((0, 0), (MLA_NOPE, LANES - MLA_NOPE - MLA_ROPE)))
    w_in_p = jnp.concatenate([w_in[:, :896], kr_blk], axis=1).astype(BF16)
    w_uq_p = _pad_heads(mla_w_uq[0], MLA_HEADS, LANES).astype(BF16)
    ukv = mla_w_ukv[0].reshape(-1, MLA_HEADS, MLA_NOPE + MLA_V)
    w_ukv_p = jnp.concatenate(
        [jnp.pad(ukv[:, :, :MLA_NOPE], ((0, 0), (0, 0), (0, LANES - MLA_NOPE))).reshape(ukv.shape[0], -1),
         ukv[:, :, MLA_NOPE:].reshape(ukv.shape[0], -1)], axis=1).astype(BF16)
    tabs = _rope_tables(s // GRID_W, n_ctx)
    m0 = modv(0)
    u_t, q, k, v = _even_in(ctx, x, m0, g(0, 0), w_in_p, mla_q_norm[0].reshape(1, -1), w_uq_p,
                            mla_kv_norm[0].reshape(1, -1), w_ukv_p, tabs,
                            float((MLA_NOPE + MLA_ROPE) ** -0.5 * math.log2(math.e)))
    a_s5, bbd, cbd = _s5_layout(s5_lam_re[0], s5_lam_im[0], s5_log_dt[0], s5_b_re[0], s5_b_im[0],
                                s5_c_re[0], s5_c_im[0])
    t_all = u_t.shape[0]
    yf, yr = _s5_scan(u_t.reshape(t_all * b, 512), a_s5, bbd, cbd, b, n_ctx // S5_TT)
    attn = _attention(q, k, v)
    h1 = _even_out(ctx, x, m0, u_t, yf.reshape(t_all, b * 512), yr.reshape(t_all, b * 512), attn,
                   s5_d[0].reshape(1, -1), s5_w_glu[0].astype(BF16), s5_b_glu[0].reshape(1, -1),
                   ev_w_out[0].astype(BF16), g(0, 1))
    h2 = _ffn(h1, m0, g(0, 2), g(0, 3), ffn_w_gate[0].astype(BF16), ffn_w_up[0].astype(BF16),
              ffn_w_down[0].astype(BF16))

    m1 = modv(1)
    gk = gla_w_gate2.shape[3]
    w_in1 = jnp.pad(od_w_in[0], ((0, 0), (0, LANES - 2 * GLA_GATE_RANK))).astype(BF16)
    wg2 = jnp.zeros((LANES, 2 * gk), F32)
    wg2 = wg2.at[:GLA_GATE_RANK, :gk].set(gla_w_gate2[0, 0])
    wg2 = wg2.at[GLA_GATE_RANK:2 * GLA_GATE_RANK, gk:].set(gla_w_gate2[0, 1]).astype(BF16)
    bg2 = gla_b_gate2[0].reshape(1, 2 * gk)
    ql, kl, vl, rl, laf, lab = _odd_in(h2, m1, g(1, 0), w_in1, wg2, bg2,
                                       float((gk // GLA_HEADS) ** -0.5))
    of, ob = _gla_scan(ql, kl, vl, laf, lab)
    h3, xl, comb_t = _odd_out(h2, m1, of, ob, rl, gla_head_norm[0].reshape(1, -1),
                              od_w_out[0].astype(BF16), g(1, 1), g(1, 2), moe_router[0].T)
    fl = _moe(xl, comb_t, moe_w_gate[0].astype(BF16), moe_w_up[0].astype(BF16),
              moe_w_down[0].astype(BF16))
    return _final(h3, fl, m1, g(1, 3))
```

```python
import functools
import math

import jax
import jax.numpy as jnp
from jax import lax
from jax.experimental import pallas as pl
from jax.experimental.pallas import tpu as pltpu

F32 = jnp.float32
BF16 = jnp.bfloat16

NORM_EPS = 1e-6
GRID_W = 64
S5_GROUP = 16
S5_STATE = 64
MLA_HEADS = 8
MLA_NOPE = 64
MLA_ROPE = 32
MLA_V = 64
ROPE_BASE = 10000.0
GLA_HEADS = 4
GLA_GATE_RANK = 16
GLA_TAU = 16.0
N_EXPERTS = 8

LANES = 128
TM = 256
S5_TT = 32
S5_CB = 512
MOE_GB = 288
MOE_FB = 288
MOE_SB = 256
MOE_FC = 512
VMEM_BIG = 56 * 1024 * 1024


def _dot(a, b):
    return jnp.dot(a, b, preferred_element_type=F32)


def _dot_nt(a, b):
    return lax.dot_general(a, b, (((1,), (1,)), ((), ())), preferred_element_type=F32)


def _dot_tn(a, b):
    return lax.dot_general(a, b, (((0,), (0,)), ((), ())), preferred_element_type=F32)


def _rms(x, g):
    return x * lax.rsqrt(jnp.mean(x * x, axis=-1, keepdims=True) + NORM_EPS) * g


def _silu(x):
    return x * jax.nn.sigmoid(x)


def _gelu_tanh(x):
    return 0.5 * x * (1.0 + jnp.tanh(math.sqrt(2.0 / math.pi) * (x + 0.044715 * (x * x * x))))


def _params(sem, vmem=None):
    return pltpu.CompilerParams(dimension_semantics=sem, vmem_limit_bytes=vmem)


def _const_spec(shape):
    nd = len(shape)
    return pl.BlockSpec(shape, lambda *_: (0,) * nd)


def _mod_kernel(c_ref, w_ref, b_ref, o_ref):
    a = _silu(c_ref[...]).astype(BF16)
    o_ref[0] = _dot(a, w_ref[0].astype(BF16)) + b_ref[0]


def _modulation(cc, mod_w, mod_b):
    depth, d, n = mod_w.shape
    rows = cc.shape[0]
    tn = n // 4
    return pl.pallas_call(
        _mod_kernel,
        out_shape=jax.ShapeDtypeStruct((depth, rows, n), F32),
        grid=(depth, n // tn),
        in_specs=[pl.BlockSpec((rows, d), lambda i, j: (0, 0)),
                  pl.BlockSpec((1, d, tn), lambda i, j: (i, 0, j)),
                  pl.BlockSpec((1, 1, tn), lambda i, j: (i, 0, j))],
        out_specs=pl.BlockSpec((1, rows, tn), lambda i, j: (i, 0, j)),
        compiler_params=_params(("arbitrary", "arbitrary"), VMEM_BIG),
        name="modulation",
    )(cc, mod_w, mod_b.reshape(depth, 1, n))


def _rope(x, tab_ref):
    return x * tab_ref[0] + pltpu.roll(x, LANES - 8, 1) * tab_ref[1] + pltpu.roll(x, 8, 1) * tab_ref[2]


def _even_in_kernel(ctx_ref, x_ref, mod_ref, g_ref, win_ref, qn_ref, wuq_ref, kvn_ref, wukv_ref,
                    tab_ref, u_ref, q_ref, k_ref, v_ref, *, q_scale):
    j = pl.program_id(1)
    xt = jnp.where(j == 0, ctx_ref[0], x_ref[0])
    m = mod_ref[0, 0]
    xn = _rms(xt, g_ref[...]) * (1.0 + m[1:2]) + m[0:1]
    z = _dot(xn.astype(BF16), win_ref[...])
    u_ref[...] = z[:, :512].astype(u_ref.dtype)
    cqn = _rms(z[:, 512:768], qn_ref[...]).astype(BF16)
    qall = _dot(cqn, wuq_ref[...])
    for h in range(MLA_HEADS):
        q_ref[0, h] = (_rope(qall[:, h * LANES:(h + 1) * LANES], tab_ref) * q_scale).astype(BF16)
    ckvn = _rms(z[:, 768:896], kvn_ref[...]).astype(BF16)
    kv = _dot(ckvn, wukv_ref[...])
    kr = _rope(z[:, 896:1024], tab_ref)
    for h in range(MLA_HEADS):
        k_ref[0, h] = (kv[:, h * LANES:(h + 1) * LANES] + kr).astype(BF16)
    v_ref[0] = kv[:, MLA_HEADS * LANES:].astype(BF16)


def _even_in(ctx, x, modv, g, w_in, q_norm, w_uq, kv_norm, w_ukv, tabs, q_scale):
    b, s, d = x.shape
    nt = s // TM + 1
    t_all = nt * TM
    return pl.pallas_call(
        functools.partial(_even_in_kernel, q_scale=q_scale),
        out_shape=(jax.ShapeDtypeStruct((t_all, b * 512), BF16),
                   jax.ShapeDtypeStruct((b, MLA_HEADS, t_all, LANES), BF16),
                   jax.ShapeDtypeStruct((b, MLA_HEADS, t_all, LANES), BF16),
                   jax.ShapeDtypeStruct((b, t_all, MLA_HEADS * MLA_V), BF16)),
        grid=(b, nt),
        in_specs=[pl.BlockSpec((1, TM, d), lambda i, j: (i, 0, 0)),
                  pl.BlockSpec((1, TM, d), lambda i, j: (i, jnp.maximum(j - 1, 0), 0)),
                  pl.BlockSpec((1, 1, 6, d), lambda i, j: (i, jnp.minimum(j, 1), 0, 0)),
                  _const_spec((1, d)),
                  _const_spec(w_in.shape),
                  _const_spec(q_norm.shape),
                  _const_spec(w_uq.shape),
                  _const_spec(kv_norm.shape),
                  _const_spec(w_ukv.shape),
                  pl.BlockSpec((3, TM, LANES), lambda i, j: (0, j, 0))],
        out_specs=(pl.BlockSpec((TM, 512), lambda i, j: (j, i)),
                   pl.BlockSpec((1, MLA_HEADS, TM, LANES), lambda i, j: (i, 0, j, 0)),
                   pl.BlockSpec((1, MLA_HEADS, TM, LANES), lambda i, j: (i, 0, j, 0)),
                   pl.BlockSpec((1, TM, MLA_HEADS * MLA_V), lambda i, j: (i, j, 0))),
        compiler_params=_params(("arbitrary", "arbitrary"), VMEM_BIG),
        name="even_in",
    )(ctx, x, modv, g, w_in, q_norm, w_uq, kv_norm, w_ukv, tabs)


def _s5_scan_kernel(uf_ref, ur_ref, a_ref, bbd_ref, cbd_ref, yf_ref, yr_ref, buf_f, buf_r, st_ref,
                    *, tt, nb):
    @pl.when(pl.program_id(0) == 0)
    def _():
        st_ref[...] = jnp.zeros_like(st_ref)

    nblk = a_ref.shape[-1] // S5_CB
    for d, (u_ref, buf) in enumerate(((uf_ref, buf_f), (ur_ref, buf_r))):
        u = u_ref[...].astype(BF16)
        for c in range(nblk):
            bu = _dot(u[:, c * LANES:(c + 1) * LANES], bbd_ref[d, c])
            buf[0, :, c * S5_CB:(c + 1) * S5_CB] = bu[:, :S5_CB]
            buf[1, :, c * S5_CB:(c + 1) * S5_CB] = bu[:, S5_CB:]

    for c in range(nblk):
        sl = slice(c * S5_CB, (c + 1) * S5_CB)
        afr = jnp.broadcast_to(a_ref[0, 0:1, sl], (nb, S5_CB))
        afi = jnp.broadcast_to(a_ref[0, 1:2, sl], (nb, S5_CB))
        arr = jnp.broadcast_to(a_ref[1, 0:1, sl], (nb, S5_CB))
        ari = jnp.broadcast_to(a_ref[1, 1:2, sl], (nb, S5_CB))

        def body(t, carry, sl=sl, afr=afr, afi=afi, arr=arr, ari=ari):
            fr, fi, rr, ri = carry
            rf = pl.ds(pl.multiple_of(t * nb, nb), nb)
            rb = pl.ds(pl.multiple_of((tt - 1 - t) * nb, nb), nb)
            nfr = afr * fr - afi * fi + buf_f[0, rf, sl]
            nfi = afr * fi + afi * fr + buf_f[1, rf, sl]
            nrr = arr * rr - ari * ri + buf_r[0, rb, sl]
            nri = arr * ri + ari * rr + buf_r[1, rb, sl]
            buf_f[0, rf, sl] = nfr
            buf_f[1, rf, sl] = nfi
            buf_r[0, rb, sl] = nrr
            buf_r[1, rb, sl] = nri
            return nfr, nfi, nrr, nri

        init = (st_ref[0, 0, :, sl], st_ref[0, 1, :, sl], st_ref[1, 0, :, sl], st_ref[1, 1, :, sl])
        fr, fi, rr, ri = lax.fori_loop(0, tt, body, init, unroll=4)
        st_ref[0, 0, :, sl] = fr
        st_ref[0, 1, :, sl] = fi
        st_ref[1, 0, :, sl] = rr
        st_ref[1, 1, :, sl] = ri

    for d, (buf, y_ref) in enumerate(((buf_f, yf_ref), (buf_r, yr_ref))):
        for c in range(nblk):
            sl = slice(c * S5_CB, (c + 1) * S5_CB)
            y_ref[:, c * LANES:(c + 1) * LANES] = (
                _dot(buf[0, :, sl].astype(BF16), cbd_ref[d, c, :S5_CB])
                + _dot(buf[1, :, sl].astype(BF16), cbd_ref[d, c, S5_CB:])).astype(y_ref.dtype)


def _s5_scan(u_t, a, bbd, cbd, nb, n_ctx_tiles):
    rows, w = u_t.shape
    tt = S5_TT
    n = rows // (tt * nb)
    nc = n_ctx_tiles
    width = a.shape[-1]

    def fwd(j):
        return (j, 0)

    def rev(j):
        return (jnp.where(j < nc, nc - 1 - j, n - 1 - j + nc), 0)

    blk = (tt * nb, w)
    return pl.pallas_call(
        functools.partial(_s5_scan_kernel, tt=tt, nb=nb),
        out_shape=(jax.ShapeDtypeStruct((rows, w), BF16), jax.ShapeDtypeStruct((rows, w), BF16)),
        grid=(n,),
        in_specs=[pl.BlockSpec(blk, fwd), pl.BlockSpec(blk, rev),
                  _const_spec(a.shape), _const_spec(bbd.shape), _const_spec(cbd.shape)],
        out_specs=(pl.BlockSpec(blk, fwd), pl.BlockSpec(blk, rev)),
        scratch_shapes=[pltpu.VMEM((2, tt * nb, width), F32),
                        pltpu.VMEM((2, tt * nb, width), F32),
                        pltpu.VMEM((2, 2, nb, width), F32)],
        compiler_params=_params(("arbitrary",), VMEM_BIG),
        name="s5_scan",
    )(u_t, u_t, a, bbd, cbd)


def _attn_kernel(q_ref, k_ref, v_ref, o_ref, *, n_ctx):
    j = pl.program_id(1)
    lane = lax.broadcasted_iota(jnp.int32, (TM, LANES), 1)

    def run(nk):
        for hp in range(MLA_HEADS // 2):
            cols = slice(hp * LANES, (hp + 1) * LANES)
            outs = []
            for h in (2 * hp, 2 * hp + 1):
                s = _dot_nt(q_ref[0, h], k_ref[0, h, :nk])
                p = jnp.exp2(s - jnp.max(s, axis=-1, keepdims=True))
                l = jnp.sum(p, axis=-1, keepdims=True)
                outs.append(_dot(p.astype(BF16), v_ref[0, :nk, cols]) / l)
            o_ref[0, :, cols] = jnp.where(lane < MLA_V, outs[0], outs[1]).astype(o_ref.dtype)

    @pl.when(j == 0)
    def _():
        run(n_ctx)

    @pl.when(j > 0)
    def _():
        run(k_ref.shape[2])


def _attention(q, k, v):
    b, h, t_all, _ = q.shape
    nt = t_all // TM
    return pl.pallas_call(
        functools.partial(_attn_kernel, n_ctx=TM),
        out_shape=jax.ShapeDtypeStruct((b, t_all, h * MLA_V), BF16),
        grid=(b, nt),
        in_specs=[pl.BlockSpec((1, h, TM, LANES), lambda i, j: (i, 0, j, 0)),
                  pl.BlockSpec((1, h, t_all, LANES), lambda i, j: (i, 0, 0, 0)),
                  pl.BlockSpec((1, t_all, h * MLA_V), lambda i, j: (i, 0, 0))],
        out_specs=pl.BlockSpec((1, TM, h * MLA_V), lambda i, j: (i, j, 0)),
        compiler_params=_params(("arbitrary", "arbitrary"), VMEM_BIG),
        name="mla_attention",
    )(q, k, v)


def _even_out_kernel(ctx_ref, x_ref, mod_ref, u_ref, yf_ref, yr_ref, a_ref, d_ref, wglu_ref, bglu_ref,
                     wout_ref, g_ref, h_ref):
    j = pl.program_id(1)
    h = jnp.where(j == 0, ctx_ref[0], x_ref[0])
    m = mod_ref[0, 0]
    u = u_ref[...].astype(F32)
    y = u * d_ref[...] + yf_ref[...].astype(F32) + yr_ref[...].astype(F32)
    act = _gelu_tanh(y)
    s5 = act * jax.nn.sigmoid(_dot(act.astype(BF16), wglu_ref[...]) + bglu_ref[...])
    mix = _dot(s5.astype(BF16), wout_ref[:512]) + _dot(a_ref[0], wout_ref[512:])
    h_ref[0] = h + m[2:3] * _rms(mix, g_ref[...])


def _even_out(ctx, x, modv, u_t, yf_t, yr_t, attn, d_skip, w_glu, b_glu, w_out, g):
    b, s, d = x.shape
    nt = s // TM + 1
    tok = pl.BlockSpec((TM, 512), lambda i, j: (j, i))
    return pl.pallas_call(
        _even_out_kernel,
        out_shape=jax.ShapeDtypeStruct((b, nt * TM, d), F32),
        grid=(b, nt),
        in_specs=[pl.BlockSpec((1, TM, d), lambda i, j: (i, 0, 0)),
                  pl.BlockSpec((1, TM, d), lambda i, j: (i, jnp.maximum(j - 1, 0), 0)),
                  pl.BlockSpec((1, 1, 6, d), lambda i, j: (i, jnp.minimum(j, 1), 0, 0)),
                  tok, tok, tok,
                  pl.BlockSpec((1, TM, 512), lambda i, j: (i, j, 0)),
                  _const_spec(d_skip.shape), _const_spec(w_glu.shape), _const_spec(b_glu.shape),
                  _const_spec(w_out.shape), _const_spec(g.shape)],
        out_specs=pl.BlockSpec((1, TM, d), lambda i, j: (i, j, 0)),
        compiler_params=_params(("arbitrary", "arbitrary"), VMEM_BIG),
        name="even_out",
    )(ctx, x, modv, u_t, yf_t, yr_t, attn, d_skip, w_glu, b_glu, w_out, g)


def _ffn_kernel(h_ref, mod_ref, g2_ref, g3_ref, wg_ref, wu_ref, wd_ref, o_ref):
    h = h_ref[0]
    m = mod_ref[0, 0]
    xn = (_rms(h, g2_ref[...]) * (1.0 + m[4:5]) + m[3:4]).astype(BF16)
    act = (_silu(_dot(xn, wg_ref[...])) * _dot(xn, wu_ref[...])).astype(BF16)
    f = _dot(act, wd_ref[...])
    o_ref[0] = h + m[5:6] * _rms(f, g3_ref[...])


def _ffn(h, modv, g2, g3, wg, wu, wd):
    b, t_all, d = h.shape
    nt = t_all // TM
    one = pl.Buffered(1)
    return pl.pallas_call(
        _ffn_kernel,
        out_shape=jax.ShapeDtypeStruct((b, t_all, d), F32),
        grid=(b, nt),
        in_specs=[pl.BlockSpec((1, TM, d), lambda i, j: (i, j, 0)),
                  pl.BlockSpec((1, 1, 6, d), lambda i, j: (i, jnp.minimum(j, 1), 0, 0)),
                  _const_spec(g2.shape), _const_spec(g3.shape),
                  pl.BlockSpec(wg.shape, lambda i, j: (0, 0), pipeline_mode=one),
                  pl.BlockSpec(wu.shape, lambda i, j: (0, 0), pipeline_mode=one),
                  pl.BlockSpec(wd.shape, lambda i, j: (0, 0), pipeline_mode=one)],
        out_specs=pl.BlockSpec((1, TM, d), lambda i, j: (i, j, 0)),
        compiler_params=_params(("arbitrary", "arbitrary"), VMEM_BIG),
        name="dense_ffn",
    )(h, modv, g2, g3, wg, wu, wd)


def _odd_in_kernel(h_ref, mod_ref, g_ref, win_ref, wg2_ref, bg2_ref, q_ref, k_ref, v_ref, r_ref,
                   laf_ref, lab_ref, *, q_scale):
    m = mod_ref[0, 0]
    xn = (_rms(h_ref[0], g_ref[...]) * (1.0 + m[1:2]) + m[0:1]).astype(BF16)
    z = _dot(xn, win_ref[...])
    q_ref[0] = (z[:, :512] * q_scale).astype(BF16)
    k_ref[0] = z[:, 512:1024].astype(BF16)
    v_ref[0] = z[:, 1024:2048].astype(BF16)
    r_ref[0] = z[:, 2048:3072].astype(BF16)
    gate = _dot(z[:, 3072:3200].astype(BF16), wg2_ref[...]) + bg2_ref[...]
    la = (jnp.minimum(gate, 0.0) - jnp.log1p(jnp.exp(-jnp.abs(gate)))) * (1.0 / GLA_TAU)
    laf_ref[0] = la[:, :512]
    lab_ref[0] = la[:, 512:]


def _odd_in(h, modv, g, w_in, w_gate2, b_gate2, q_scale):
    b, t_all, d = h.shape
    nt = t_all // TM

    def tok(w):
        return pl.BlockSpec((1, TM, w), lambda i, j: (i, j, 0))

    def out(w, dt):
        return jax.ShapeDtypeStruct((b, t_all, w), dt)

    return pl.pallas_call(
        functools.partial(_odd_in_kernel, q_scale=q_scale),
        out_shape=(out(512, BF16), out(512, BF16), out(1024, BF16), out(1024, BF16),
                   out(512, F32), out(512, F32)),
        grid=(b, nt),
        in_specs=[tok(d),
                  pl.BlockSpec((1, 1, 6, d), lambda i, j: (i, jnp.minimum(j, 1), 0, 0)),
                  _const_spec(g.shape), _const_spec(w_in.shape), _const_spec(w_gate2.shape),
                  _const_spec(b_gate2.shape)],
        out_specs=(tok(512), tok(512), tok(1024), tok(1024), tok(512), tok(512)),
        compiler_params=_params(("arbitrary", "arbitrary"), VMEM_BIG),
        name="odd_in",
    )(h, modv, g, w_in, w_gate2, b_gate2)


def _gla_kernel(qf_ref, kf_ref, vf_ref, laf_ref, qb_ref, kb_ref, vb_ref, lab_ref, of_ref, ob_ref,
                sf_ref, sb_ref):
    @pl.when(pl.program_id(1) == 0)
    def _():
        sf_ref[...] = jnp.zeros_like(sf_ref)
        sb_ref[...] = jnp.zeros_like(sb_ref)

    n = qf_ref.shape[1]
    dk = qf_ref.shape[2] // GLA_HEADS
    dv = vf_ref.shape[2] // GLA_HEADS
    row = lax.broadcasted_iota(jnp.int32, (n, n), 0)
    col = lax.broadcasted_iota(jnp.int32, (n, n), 1)
    mid = n // 2
    dirs = ((qf_ref, kf_ref, vf_ref, laf_ref, of_ref, sf_ref, col <= row, n - 1, mid - 1),
            (qb_ref, kb_ref, vb_ref, lab_ref, ob_ref, sb_ref, col >= row, 0, mid))
    for q_ref, k_ref, v_ref, la_ref, o_ref, s_ref, keep, i_tot, i_mid in dirs:
        la = la_ref[0]
        hi = la.astype(BF16)
        lo = (la - hi.astype(F32)).astype(BF16)
        tri = keep.astype(BF16)
        cum = _dot(tri, hi) + _dot(tri, lo)
        tot = cum[i_tot:i_tot + 1]
        cm = cum[i_mid:i_mid + 1]
        qe = q_ref[0].astype(F32) * jnp.exp(cum - cm)
        ke = k_ref[0].astype(F32) * jnp.exp(cm - cum)
        qi = (qe * jnp.exp(cm)).astype(BF16)
        k2 = (ke * jnp.exp(tot - cm)).astype(BF16)
        e_tot = jnp.exp(tot)
        qe = qe.astype(BF16)
        ke = ke.astype(BF16)
        for h in range(GLA_HEADS):
            ks = slice(h * dk, (h + 1) * dk)
            vs = slice(h * dv, (h + 1) * dv)
            v = v_ref[0, :, vs]
            sc = jnp.where(keep, _dot_nt(qe[:, ks], ke[:, ks]), 0.0).astype(BF16)
            st = s_ref[h]
            o_ref[0, :, vs] = (_dot(sc, v) + _dot_nt(qi[:, ks], st.astype(BF16))).astype(o_ref.dtype)
            s_ref[h] = st * e_tot[:, ks] + _dot_tn(v, k2[:, ks])


def _gla_scan(q, k, v, laf, lab):
    b, t_all, kw = q.shape
    vw = v.shape[2]
    n = t_all // TM

    def fwd(i, j):
        return (i, j, 0)

    def rev(i, j):
        return (i, jnp.where(j == 0, 0, n - j), 0)

    def spec(w, im):
        return pl.BlockSpec((1, TM, w), im)

    return pl.pallas_call(
        _gla_kernel,
        out_shape=(jax.ShapeDtypeStruct((b, t_all, vw), BF16), jax.ShapeDtypeStruct((b, t_all, vw), BF16)),
        grid=(b, n),
        in_specs=[spec(kw, fwd), spec(kw, fwd), spec(vw, fwd), spec(kw, fwd),
                  spec(kw, rev), spec(kw, rev), spec(vw, rev), spec(kw, rev)],
        out_specs=(spec(vw, fwd), spec(vw, rev)),
        scratch_shapes=[pltpu.VMEM((GLA_HEADS, vw // GLA_HEADS, kw // GLA_HEADS), F32),
                        pltpu.VMEM((GLA_HEADS, vw // GLA_HEADS, kw // GLA_HEADS), F32)],
        compiler_params=_params(("arbitrary", "arbitrary"), VMEM_BIG),
        name="gla_scan",
    )(q, k, v, laf, q, k, v, lab)


def _odd_out_kernel(h_ref, mod_ref, of_ref, ob_ref, r_ref, hn_ref, wout_ref, g1_ref, g2_ref, rt_ref,
                    h3_ref, xl_ref, comb_ref):
    m = mod_ref[0, 0]
    o = of_ref[0].astype(F32) + ob_ref[0].astype(F32)
    dv = hn_ref.shape[1]
    parts = [_rms(o[:, h * dv:(h + 1) * dv], hn_ref[...]) for h in range(GLA_HEADS)]
    on = jnp.concatenate(parts, axis=1) * _silu(r_ref[0].astype(F32))
    h3 = h_ref[0] + m[2:3] * _rms(_dot(on.astype(BF16), wout_ref[...]), g1_ref[...])
    h3_ref[0] = h3
    xl = _rms(h3, g2_ref[...]) * (1.0 + m[4:5]) + m[3:4]
    xl_ref[0] = xl.astype(BF16)
    xh = xl.astype(BF16)
    xo = (xl - xh.astype(F32)).astype(BF16)
    rt = rt_ref[...]
    rh = rt.astype(BF16)
    ro = (rt - rh.astype(F32)).astype(BF16)
    logit = _dot_nt(rh, xh) + (_dot_nt(rh, xo) + _dot_nt(ro, xh))
    ne = logit.shape[0]
    eid = lax.broadcasted_iota(jnp.int32, logit.shape, 0)
    m1 = jnp.max(logit, axis=0, keepdims=True)
    i1 = jnp.min(jnp.where(logit == m1, eid, ne), axis=0, keepdims=True)
    rest = jnp.where(eid == i1, -jnp.inf, logit)
    m2 = jnp.max(rest, axis=0, keepdims=True)
    i2 = jnp.min(jnp.where(rest == m2, eid, ne), axis=0, keepdims=True)
    e2 = jnp.exp(m2 - m1)
    w1 = 1.0 / (1.0 + e2)
    comb_ref[...] = jnp.where(eid == i1, w1, 0.0) + jnp.where(eid == i2, e2 * w1, 0.0)


def _odd_out(h, modv, of, ob, r, head_norm, w_out, g1, g2, router_t):
    b, t_all, d = h.shape
    nl = t_all // TM - 1

    def tok(w):
        return pl.BlockSpec((1, TM, w), lambda i, j: (i, j + 1, 0))

    def lat(w):
        return pl.BlockSpec((1, TM, w), lambda i, j: (i, j, 0))

    ne = router_t.shape[0]
    return pl.pallas_call(
        _odd_out_kernel,
        out_shape=(jax.ShapeDtypeStruct((b, nl * TM, d), F32),
                   jax.ShapeDtypeStruct((b, nl * TM, d), BF16),
                   jax.ShapeDtypeStruct((ne, b * nl * TM), F32)),
        grid=(b, nl),
        in_specs=[tok(d),
                  pl.BlockSpec((1, 1, 6, d), lambda i, j: (i, 1, 0, 0)),
                  tok(of.shape[2]), tok(ob.shape[2]), tok(r.shape[2]),
                  _const_spec(head_norm.shape), _const_spec(w_out.shape), _const_spec(g1.shape),
                  _const_spec(g2.shape), _const_spec(router_t.shape)],
        out_specs=(lat(d), lat(d), pl.BlockSpec((ne, TM), lambda i, j: (0, i * nl + j))),
        compiler_params=_params(("arbitrary", "arbitrary"), VMEM_BIG),
        name="odd_out",
    )(h, modv, of, ob, r, head_norm, w_out, g1, g2, router_t)


def _ceil_count(n, step, cap):
    total = jnp.int32(0)
    for k in range(0, cap, step):
        total = total + (n > k).astype(jnp.int32)
    return total


def _round16(n):
    return lax.shift_left(lax.shift_right_logical(n + 15, 4), 4)


def _moe_kernel(x_ref, ct_ref, wg_ref, wu_ref, wd_ref, y_ref, xs_ref, acc_ref, tri_ref, rank_ref,
                cnt_ref, *, st, ns):
    b = pl.program_id(0)
    e = pl.program_id(1)
    c = pl.program_id(2)
    last_c = pl.num_programs(2) - 1
    ne = ct_ref.shape[0]
    cap = xs_ref.shape[0]
    trip = 2 * MOE_FB

    @pl.when((b == 0) & (e == 0) & (c == 0))
    def _():
        r = lax.broadcasted_iota(jnp.int32, (st, st), 0)
        cc = lax.broadcasted_iota(jnp.int32, (st, st), 1)
        tri_ref[...] = (r < cc).astype(BF16)

    @pl.when((e == 0) & (c == 0))
    def _():
        y_ref[...] = jnp.zeros_like(y_ref)
        xs_ref[...] = jnp.zeros_like(xs_ref)
        acc_ref[...] = jnp.zeros_like(acc_ref)
        for s in range(ns):
            sel = ct_ref[:, s * st:(s + 1) * st] > 0.0
            m16 = jnp.concatenate([sel.astype(F32), jnp.zeros((16 - ne, st), F32)], axis=0).astype(BF16)
            rank = _dot(m16, tri_ref[...])[:ne]
            rank_ref[:, s * st:(s + 1) * st] = jnp.where(sel, rank, -1.0)

    def routed(s):
        cols = slice(s * st, (s + 1) * st)
        return ct_ref[pl.ds(e, 1), cols], rank_ref[pl.ds(e, 1), cols]

    def row_ids(i, rows):
        return (lax.broadcasted_iota(jnp.int32, (rows, st), 0) + i * rows).astype(F32)

    @pl.when(c == 0)
    def _gather():
        off = jnp.int32(0)
        total = jnp.int32(0)
        for s in range(ns):
            w_row, rank = routed(s)
            cnt = jnp.sum((w_row > 0.0).astype(F32)).astype(jnp.int32)
            cnt_ref[s] = cnt

            def gather_block(i, carry, s=s, rank=rank, off=off):
                onehot = jnp.where(rank == row_ids(i, MOE_GB), 1.0, 0.0).astype(BF16)
                rows = _dot(onehot, x_ref[0, s * st:(s + 1) * st, :])
                xs_ref[pl.ds(pl.multiple_of(off + i * MOE_GB, 16), MOE_GB), :] = rows.astype(BF16)
                return carry

            lax.fori_loop(0, _ceil_count(cnt, MOE_GB, st), gather_block, 0)
            total = off + cnt
            off = off + _round16(cnt)
        cnt_ref[ns] = total

        def zero_trip(i, carry):
            acc_ref[pl.ds(pl.multiple_of(i * trip, 16), trip), :] = jnp.zeros((trip, acc_ref.shape[1]), F32)
            return carry

        lax.fori_loop(0, _ceil_count(total, trip, cap), zero_trip, 0)

    def ffn_trip(i, carry):
        for k in range(2):
            rows = pl.ds(pl.multiple_of(i * trip + k * MOE_FB, 16), MOE_FB)
            xb = xs_ref[rows, :]
            act = (_silu(_dot(xb, wg_ref[0, 0])) * _dot(xb, wu_ref[0, 0])).astype(BF16)
            acc_ref[rows, :] += _dot(act, wd_ref[0])
        return carry

    lax.fori_loop(0, _ceil_count(cnt_ref[ns], trip, cap), ffn_trip, 0)

    @pl.when(c == last_c)
    def _scatter():
        off = jnp.int32(0)
        for s in range(ns):
            w_row, rank = routed(s)
            cnt = cnt_ref[s]

            def scatter_block(i, carry, s=s, w_row=w_row, rank=rank, off=off):
                weighted = jnp.where(rank == row_ids(i, MOE_SB), w_row, 0.0).astype(BF16)
                rows = acc_ref[pl.ds(pl.multiple_of(off + i * MOE_SB, 16), MOE_SB), :].astype(BF16)
                y_ref[0, s * st:(s + 1) * st, :] += _dot_tn(weighted, rows)
                return carry

            lax.fori_loop(0, _ceil_count(cnt, MOE_SB, st), scatter_block, 0)
            off = off + _round16(cnt)


def _moe(xl, comb_t, wg, wu, wd):
    b, t, d = xl.shape
    ne, nchunk, _, fc = wg.shape
    st = min(1024, t // 2)
    ns = t // st
    trip = 2 * MOE_FB
    need = max(t, (ns - 1) * st + -(-st // MOE_GB) * MOE_GB)
    cap = -(-need // trip) * trip
    return pl.pallas_call(
        functools.partial(_moe_kernel, st=st, ns=ns),
        out_shape=jax.ShapeDtypeStruct((b, t, d), F32),
        grid=(b, ne, nchunk),
        in_specs=[pl.BlockSpec((1, t, d), lambda i, e, c: (i, 0, 0)),
                  pl.BlockSpec((ne, t), lambda i, e, c: (0, i)),
                  pl.BlockSpec((1, 1, d, fc), lambda i, e, c: (e, c, 0, 0)),
                  pl.BlockSpec((1, 1, d, fc), lambda i, e, c: (e, c, 0, 0)),
                  pl.BlockSpec((1, fc, d), lambda i, e, c: (e, c, 0))],
        out_specs=pl.BlockSpec((1, t, d), lambda i, e, c: (i, 0, 0)),
        scratch_shapes=[pltpu.VMEM((cap, d), BF16), pltpu.VMEM((cap, d), F32),
                        pltpu.VMEM((st, st), BF16), pltpu.VMEM((ne, t), F32),
                        pltpu.SMEM((ns + 1,), jnp.int32)],
        compiler_params=_params(("arbitrary", "arbitrary", "arbitrary"), VMEM_BIG),
        name="moe_ffn",
    )(xl, comb_t, wg, wu, wd)


def _final_kernel(h_ref, f_ref, mod_ref, g_ref, o_ref):
    m = mod_ref[0, 0]
    o_ref[0] = h_ref[0] + m[5:6] * _rms(f_ref[0], g_ref[...])


def _final(h3, fl, modv, g):
    b, t, d = h3.shape
    tok = pl.BlockSpec((1, TM, d), lambda i, j: (i, j, 0))
    return pl.pallas_call(
        _final_kernel,
        out_shape=jax.ShapeDtypeStruct((b, t, d), F32),
        grid=(b, t // TM),
        in_specs=[tok, tok, pl.BlockSpec((1, 1, 6, d), lambda i, j: (i, 1, 0, 0)), _const_spec(g.shape)],
        out_specs=tok,
        compiler_params=_params(("arbitrary", "arbitrary")),
        name="final_residual",
    )(h3, fl, modv, g)


def _rope_tables(n_rows, n_ctx):
    half = MLA_ROPE // 2
    inv = 1.0 / (ROPE_BASE ** (jnp.arange(0, half, 2, dtype=F32) / half))
    rows = jnp.repeat(jnp.arange(n_rows, dtype=F32), GRID_W)
    cols = jnp.tile(jnp.arange(GRID_W, dtype=F32), n_rows)
    ang_r = rows[:, None] * inv
    ang_c = cols[:, None] * inv
    cr, sr, cc, sc = jnp.cos(ang_r), jnp.sin(ang_r), jnp.cos(ang_c), jnp.sin(ang_c)
    t = rows.shape[0]
    one = jnp.ones((t, MLA_NOPE), F32)
    zero = jnp.zeros((t, MLA_NOPE), F32)
    z8 = jnp.zeros((t, half // 2), F32)
    pad1 = jnp.ones((t, LANES - MLA_NOPE - MLA_ROPE), F32)
    pad0 = jnp.zeros((t, LANES - MLA_NOPE - MLA_ROPE), F32)
    cos = jnp.concatenate([one, cr, cr, cc, cc, pad1], axis=1)
    s_up = jnp.concatenate([zero, -sr, z8, -sc, z8, pad0], axis=1)
    s_dn = jnp.concatenate([zero, z8, sr, z8, sc, pad0], axis=1)
    tab = jnp.stack([cos, s_up, s_dn])
    ctx = jnp.stack([jnp.ones((n_ctx, LANES), F32), jnp.zeros((n_ctx, LANES), F32),
                     jnp.zeros((n_ctx, LANES), F32)])
    return jnp.concatenate([ctx, tab], axis=1)


def _s5_discretize(lam_re, lam_im, log_dt, b_re, b_im):
    dt = jnp.exp(log_dt)[:, None]
    mag = jnp.exp(lam_re * dt)
    abar_re = mag * jnp.cos(lam_im * dt)
    abar_im = mag * jnp.sin(lam_im * dt)
    den = lam_re * lam_re + lam_im * lam_im
    nr = abar_re - 1.0
    coef_re = (nr * lam_re + abar_im * lam_im) / den
    coef_im = (abar_im * lam_re - nr * lam_im) / den
    bbar_re = coef_re[..., None] * b_re - coef_im[..., None] * b_im
    bbar_im = coef_re[..., None] * b_im + coef_im[..., None] * b_re
    return abar_re, abar_im, bbar_re, bbar_im


def _block_diag(m, per):
    g, r, c = m.shape
    eye = jnp.eye(per, dtype=m.dtype)
    m = m.reshape(g // per, per, r, c)
    return jnp.einsum("sarc,ab->sarbc", m, eye).reshape(g // per, per * r, per * c)


def _s5_layout(lam_re, lam_im, log_dt, b_re, b_im, c_re, c_im):
    per = LANES // S5_GROUP
    a, bbd, cbd = [], [], []
    for d in range(2):
        ar, ai, br, bi = _s5_discretize(lam_re[d], lam_im[d], log_dt[d], b_re[d], b_im[d])
        a.append(jnp.stack([ar.reshape(-1), ai.reshape(-1)]))
        bbd.append(jnp.concatenate([_block_diag(br.transpose(0, 2, 1), per),
                                    _block_diag(bi.transpose(0, 2, 1), per)], axis=2))
        cbd.append(jnp.concatenate([_block_diag(c_re[d].transpose(0, 2, 1), per),
                                    _block_diag(-c_im[d].transpose(0, 2, 1), per)], axis=1))
    return jnp.stack(a), jnp.stack(bbd).astype(BF16), jnp.stack(cbd).astype(BF16)


def _pad_heads(w, heads, width):
    k = w.shape[0]
    w = w.reshape(k, heads, -1)
    return jnp.pad(w, ((0, 0), (0, 0), (0, width - w.shape[2]))).reshape(k, heads * width)


def kernel(x, c, ctx, c_ctx, mod_w, mod_b, norm_g, ev_w_in, s5_lam_re, s5_lam_im, s5_log_dt, s5_b_re, s5_b_im, s5_c_re, s5_c_im, s5_d, s5_w_glu, s5_b_glu, mla_q_norm, mla_w_uq, mla_kv_norm, mla_w_ukv, ev_w_out, ffn_w_gate, ffn_w_up, ffn_w_down, od_w_in, gla_w_gate2, gla_b_gate2, gla_head_norm, od_w_out, moe_router, moe_w_gate, moe_w_up, moe_w_down):
    b, s, d = x.shape
    n_ctx = ctx.shape[1]
    assert n_ctx == TM and s % TM == 0 and mod_w.shape[0] == 2

    rows = -(-(b + 1) // 8) * 8
    cc = jnp.concatenate([c, c_ctx[None], jnp.zeros((rows - b - 1, d), F32)], axis=0)
    mods = _modulation(cc, mod_w, mod_b)

    def modv(i):
        lat = mods[i, :b].reshape(b, 1, 6, d)
        cx = jnp.broadcast_to(mods[i, b].reshape(1, 1, 6, d), (b, 1, 6, d))
        return jnp.concatenate([cx, lat], axis=1)

    def g(i, k):
        return norm_g[i, k].reshape(1, d)

    w_in = ev_w_in[0]
    kr_blk = jnp.pad(w_in[:, 896:928], ((0, 0), (MLA_NOPE, LANES - MLA_NOPE - MLA_ROPE)))
    w_in_p = jnp.concatenate([w_in[:, :896], kr_blk], axis=1).astype(BF16)
    w_uq_p = _pad_heads(mla_w_uq[0], MLA_HEADS, LANES).astype(BF16)
    ukv = mla_w_ukv[0].reshape(-1, MLA_HEADS, MLA_NOPE + MLA_V)
    w_ukv_p = jnp.concatenate(
        [jnp.pad(ukv[:, :, :MLA_NOPE], ((0, 0), (0, 0), (0, LANES - MLA_NOPE))).reshape(ukv.shape[0], -1),
         ukv[:, :, MLA_NOPE:].reshape(ukv.shape[0], -1)], axis=1).astype(BF16)
    tabs = _rope_tables(s // GRID_W, n_ctx)
    m0 = modv(0)
    u_t, q, k, v = _even_in(ctx, x, m0, g(0, 0), w_in_p, mla_q_norm[0].reshape(1, -1), w_uq_p,
                            mla_kv_norm[0].reshape(1, -1), w_ukv_p, tabs,
                            float((MLA_NOPE + MLA_ROPE) ** -0.5 * math.log2(math.e)))
    a_s5, bbd, cbd = _s5_layout(s5_lam_re[0], s5_lam_im[0], s5_log_dt[0], s5_b_re[0], s5_b_im[0],
                                s5_c_re[0], s5_c_im[0])
    t_all = u_t.shape[0]
    yf, yr = _s5_scan(u_t.reshape(t_all * b, 512), a_s5, bbd, cbd, b, n_ctx // S5_TT)
    attn = _attention(q, k, v)
    h1 = _even_out(ctx, x, m0, u_t, yf.reshape(t_all, b * 512), yr.reshape(t_all, b * 512), attn,
                   s5_d[0].reshape(1, -1), s5_w_glu[0].astype(BF16), s5_b_glu[0].reshape(1, -1),
                   ev_w_out[0].astype(BF16), g(0, 1))
    h2 = _ffn(h1, m0, g(0, 2), g(0, 3), ffn_w_gate[0].astype(BF16), ffn_w_up[0].astype(BF16),
              ffn_w_down[0].astype(BF16))

    m1 = modv(1)
    gk = gla_w_gate2.shape[3]
    w_in1 = jnp.pad(od_w_in[0], ((0, 0), (0, LANES - 2 * GLA_GATE_RANK))).astype(BF16)
    wg2 = jnp.zeros((LANES, 2 * gk), F32)
    wg2 = wg2.at[:GLA_GATE_RANK, :gk].set(gla_w_gate2[0, 0])
    wg2 = wg2.at[GLA_GATE_RANK:2 * GLA_GATE_RANK, gk:].set(gla_w_gate2[0, 1]).astype(BF16)
    bg2 = gla_b_gate2[0].reshape(1, 2 * gk)
    ql, kl, vl, rl, laf, lab = _odd_in(h2, m1, g(1, 0), w_in1, wg2, bg2,
                                       float((gk // GLA_HEADS) ** -0.5))
    of, ob = _gla_scan(ql, kl, vl, laf, lab)
    h3, xl, comb_t = _odd_out(h2, m1, of, ob, rl, gla_head_norm[0].reshape(1, -1),
                              od_w_out[0].astype(BF16), g(1, 1), g(1, 2), moe_router[0].T)
    def chunked(w):
        ne, dd, f = w.shape
        return w.astype(BF16).reshape(ne, dd, f // MOE_FC, MOE_FC).transpose(0, 2, 1, 3)

    fl = _moe(xl, comb_t, chunked(moe_w_gate[0]), chunked(moe_w_up[0]), moe_w_down[0].astype(BF16))
    return _final(h3, fl, m1, g(1, 3))
```

```python
import functools
import math

import jax
import jax.numpy as jnp
from jax import lax
from jax.experimental import pallas as pl
from jax.experimental.pallas import tpu as pltpu

F32 = jnp.float32
BF16 = jnp.bfloat16

NORM_EPS = 1e-6
GRID_W = 64
S5_GROUP = 16
S5_STATE = 64
MLA_HEADS = 8
MLA_NOPE = 64
MLA_ROPE = 32
MLA_V = 64
ROPE_BASE = 10000.0
GLA_HEADS = 4
GLA_GATE_RANK = 16
GLA_TAU = 16.0
N_EXPERTS = 8

LANES = 128
TM = 256
S5_TT = 32
S5_CB = 512
MOE_GB = 288
MOE_FB = 288
MOE_SB = 256
MOE_FC = 896
VMEM_BIG = 56 * 1024 * 1024


def _dot(a, b):
    return jnp.dot(a, b, preferred_element_type=F32)


def _dot_nt(a, b):
    return lax.dot_general(a, b, (((1,), (1,)), ((), ())), preferred_element_type=F32)


def _dot_tn(a, b):
    return lax.dot_general(a, b, (((0,), (0,)), ((), ())), preferred_element_type=F32)


def _rms(x, g):
    return x * lax.rsqrt(jnp.mean(x * x, axis=-1, keepdims=True) + NORM_EPS) * g


def _silu(x):
    return x * jax.nn.sigmoid(x)


def _gelu_tanh(x):
    return 0.5 * x * (1.0 + jnp.tanh(math.sqrt(2.0 / math.pi) * (x + 0.044715 * (x * x * x))))


def _params(sem, vmem=None):
    return pltpu.CompilerParams(dimension_semantics=sem, vmem_limit_bytes=vmem)


def _const_spec(shape):
    nd = len(shape)
    return pl.BlockSpec(shape, lambda *_: (0,) * nd)


def _mod_kernel(c_ref, w_ref, b_ref, o_ref):
    a = _silu(c_ref[...]).astype(BF16)
    o_ref[0] = _dot(a, w_ref[0].astype(BF16)) + b_ref[0]


def _modulation(cc, mod_w, mod_b):
    depth, d, n = mod_w.shape
    rows = cc.shape[0]
    tn = n // 4
    return pl.pallas_call(
        _mod_kernel,
        out_shape=jax.ShapeDtypeStruct((depth, rows, n), F32),
        grid=(depth, n // tn),
        in_specs=[pl.BlockSpec((rows, d), lambda i, j: (0, 0)),
                  pl.BlockSpec((1, d, tn), lambda i, j: (i, 0, j)),
                  pl.BlockSpec((1, 1, tn), lambda i, j: (i, 0, j))],
        out_specs=pl.BlockSpec((1, rows, tn), lambda i, j: (i, 0, j)),
        compiler_params=_params(("arbitrary", "arbitrary"), VMEM_BIG),
        name="modulation",
    )(cc, mod_w, mod_b.reshape(depth, 1, n))


def _rope(x, tab_ref):
    return x * tab_ref[0] + pltpu.roll(x, LANES - 8, 1) * tab_ref[1] + pltpu.roll(x, 8, 1) * tab_ref[2]


def _even_in_kernel(ctx_ref, x_ref, mod_ref, g_ref, win_ref, qn_ref, wuq_ref, kvn_ref, wukv_ref,
                    tab_ref, u_ref, q_ref, k_ref, v_ref, *, q_scale):
    j = pl.program_id(1)
    xt = jnp.where(j == 0, ctx_ref[0], x_ref[0])
    m = mod_ref[0, 0]
    xn = _rms(xt, g_ref[...]) * (1.0 + m[1:2]) + m[0:1]
    z = _dot(xn.astype(BF16), win_ref[...])
    u_ref[...] = z[:, :512].astype(u_ref.dtype)
    cqn = _rms(z[:, 512:768], qn_ref[...]).astype(BF16)
    qall = _dot(cqn, wuq_ref[...])
    for h in range(MLA_HEADS):
        q_ref[0, h] = (_rope(qall[:, h * LANES:(h + 1) * LANES], tab_ref) * q_scale).astype(BF16)
    ckvn = _rms(z[:, 768:896], kvn_ref[...]).astype(BF16)
    kv = _dot(ckvn, wukv_ref[...])
    kr = _rope(z[:, 896:1024], tab_ref)
    for h in range(MLA_HEADS):
        k_ref[0, h] = (kv[:, h * LANES:(h + 1) * LANES] + kr).astype(BF16)
    v_ref[0] = kv[:, MLA_HEADS * LANES:].astype(BF16)


def _even_in(ctx, x, modv, g, w_in, q_norm, w_uq, kv_norm, w_ukv, tabs, q_scale):
    b, s, d = x.shape
    nt = s // TM + 1
    t_all = nt * TM
    return pl.pallas_call(
        functools.partial(_even_in_kernel, q_scale=q_scale),
        out_shape=(jax.ShapeDtypeStruct((t_all, b * 512), BF16),
                   jax.ShapeDtypeStruct((b, MLA_HEADS, t_all, LANES), BF16),
                   jax.ShapeDtypeStruct((b, MLA_HEADS, t_all, LANES), BF16),
                   jax.ShapeDtypeStruct((b, t_all, MLA_HEADS * MLA_V), BF16)),
        grid=(b, nt),
        in_specs=[pl.BlockSpec((1, TM, d), lambda i, j: (i, 0, 0)),
                  pl.BlockSpec((1, TM, d), lambda i, j: (i, jnp.maximum(j - 1, 0), 0)),
                  pl.BlockSpec((1, 1, 6, d), lambda i, j: (i, jnp.minimum(j, 1), 0, 0)),
                  _const_spec((1, d)),
                  _const_spec(w_in.shape),
                  _const_spec(q_norm.shape),
                  _const_spec(w_uq.shape),
                  _const_spec(kv_norm.shape),
                  _const_spec(w_ukv.shape),
                  pl.BlockSpec((3, TM, LANES), lambda i, j: (0, j, 0))],
        out_specs=(pl.BlockSpec((TM, 512), lambda i, j: (j, i)),
                   pl.BlockSpec((1, MLA_HEADS, TM, LANES), lambda i, j: (i, 0, j, 0)),
                   pl.BlockSpec((1, MLA_HEADS, TM, LANES), lambda i, j: (i, 0, j, 0)),
                   pl.BlockSpec((1, TM, MLA_HEADS * MLA_V), lambda i, j: (i, j, 0))),
        compiler_params=_params(("arbitrary", "arbitrary"), VMEM_BIG),
        name="even_in",
    )(ctx, x, modv, g, w_in, q_norm, w_uq, kv_norm, w_ukv, tabs)


def _s5_scan_kernel(uf_ref, ur_ref, a_ref, bbd_ref, cbd_ref, yf_ref, yr_ref, buf_f, buf_r, st_ref,
                    *, tt, nb):
    @pl.when(pl.program_id(0) == 0)
    def _():
        st_ref[...] = jnp.zeros_like(st_ref)

    nblk = a_ref.shape[-1] // S5_CB
    for d, (u_ref, buf) in enumerate(((uf_ref, buf_f), (ur_ref, buf_r))):
        u = u_ref[...].astype(BF16)
        for c in range(nblk):
            bu = _dot(u[:, c * LANES:(c + 1) * LANES], bbd_ref[d, c])
            buf[0, :, c * S5_CB:(c + 1) * S5_CB] = bu[:, :S5_CB]
            buf[1, :, c * S5_CB:(c + 1) * S5_CB] = bu[:, S5_CB:]

    for c in range(nblk):
        sl = slice(c * S5_CB, (c + 1) * S5_CB)
        afr = jnp.broadcast_to(a_ref[0, 0:1, sl], (nb, S5_CB))
        afi = jnp.broadcast_to(a_ref[0, 1:2, sl], (nb, S5_CB))
        arr = jnp.broadcast_to(a_ref[1, 0:1, sl], (nb, S5_CB))
        ari = jnp.broadcast_to(a_ref[1, 1:2, sl], (nb, S5_CB))

        def body(t, carry, sl=sl, afr=afr, afi=afi, arr=arr, ari=ari):
            fr, fi, rr, ri = carry
            rf = pl.ds(pl.multiple_of(t * nb, nb), nb)
            rb = pl.ds(pl.multiple_of((tt - 1 - t) * nb, nb), nb)
            nfr = afr * fr - afi * fi + buf_f[0, rf, sl]
            nfi = afr * fi + afi * fr + buf_f[1, rf, sl]
            nrr = arr * rr - ari * ri + buf_r[0, rb, sl]
            nri = arr * ri + ari * rr + buf_r[1, rb, sl]
            buf_f[0, rf, sl] = nfr
            buf_f[1, rf, sl] = nfi
            buf_r[0, rb, sl] = nrr
            buf_r[1, rb, sl] = nri
            return nfr, nfi, nrr, nri

        init = (st_ref[0, 0, :, sl], st_ref[0, 1, :, sl], st_ref[1, 0, :, sl], st_ref[1, 1, :, sl])
        fr, fi, rr, ri = lax.fori_loop(0, tt, body, init, unroll=4)
        st_ref[0, 0, :, sl] = fr
        st_ref[0, 1, :, sl] = fi
        st_ref[1, 0, :, sl] = rr
        st_ref[1, 1, :, sl] = ri

    for d, (buf, y_ref) in enumerate(((buf_f, yf_ref), (buf_r, yr_ref))):
        for c in range(nblk):
            sl = slice(c * S5_CB, (c + 1) * S5_CB)
            y_ref[:, c * LANES:(c + 1) * LANES] = (
                _dot(buf[0, :, sl].astype(BF16), cbd_ref[d, c, :S5_CB])
                + _dot(buf[1, :, sl].astype(BF16), cbd_ref[d, c, S5_CB:])).astype(y_ref.dtype)


def _s5_scan(u_t, a, bbd, cbd, nb, n_ctx_tiles):
    rows, w = u_t.shape
    tt = S5_TT
    n = rows // (tt * nb)
    nc = n_ctx_tiles
    width = a.shape[-1]

    def fwd(j):
        return (j, 0)

    def rev(j):
        return (jnp.where(j < nc, nc - 1 - j, n - 1 - j + nc), 0)

    blk = (tt * nb, w)
    return pl.pallas_call(
        functools.partial(_s5_scan_kernel, tt=tt, nb=nb),
        out_shape=(jax.ShapeDtypeStruct((rows, w), BF16), jax.ShapeDtypeStruct((rows, w), BF16)),
        grid=(n,),
        in_specs=[pl.BlockSpec(blk, fwd), pl.BlockSpec(blk, rev),
                  _const_spec(a.shape), _const_spec(bbd.shape), _const_spec(cbd.shape)],
        out_specs=(pl.BlockSpec(blk, fwd), pl.BlockSpec(blk, rev)),
        scratch_shapes=[pltpu.VMEM((2, tt * nb, width), F32),
                        pltpu.VMEM((2, tt * nb, width), F32),
                        pltpu.VMEM((2, 2, nb, width), F32)],
        compiler_params=_params(("arbitrary",), VMEM_BIG),
        name="s5_scan",
    )(u_t, u_t, a, bbd, cbd)


def _attn_kernel(q_ref, k_ref, v_ref, o_ref, *, n_ctx):
    j = pl.program_id(1)
    lane = lax.broadcasted_iota(jnp.int32, (TM, LANES), 1)

    def run(nk):
        for hp in range(MLA_HEADS // 2):
            cols = slice(hp * LANES, (hp + 1) * LANES)
            outs = []
            for h in (2 * hp, 2 * hp + 1):
                s = _dot_nt(q_ref[0, h], k_ref[0, h, :nk])
                p = jnp.exp2(s - jnp.max(s, axis=-1, keepdims=True))
                l = jnp.sum(p, axis=-1, keepdims=True)
                outs.append(_dot(p.astype(BF16), v_ref[0, :nk, cols]) / l)
            o_ref[0, :, cols] = jnp.where(lane < MLA_V, outs[0], outs[1]).astype(o_ref.dtype)

    @pl.when(j == 0)
    def _():
        run(n_ctx)

    @pl.when(j > 0)
    def _():
        run(k_ref.shape[2])


def _attention(q, k, v):
    b, h, t_all, _ = q.shape
    nt = t_all // TM
    return pl.pallas_call(
        functools.partial(_attn_kernel, n_ctx=TM),
        out_shape=jax.ShapeDtypeStruct((b, t_all, h * MLA_V), BF16),
        grid=(b, nt),
        in_specs=[pl.BlockSpec((1, h, TM, LANES), lambda i, j: (i, 0, j, 0)),
                  pl.BlockSpec((1, h, t_all, LANES), lambda i, j: (i, 0, 0, 0)),
                  pl.BlockSpec((1, t_all, h * MLA_V), lambda i, j: (i, 0, 0))],
        out_specs=pl.BlockSpec((1, TM, h * MLA_V), lambda i, j: (i, j, 0)),
        compiler_params=_params(("arbitrary", "arbitrary"), VMEM_BIG),
        name="mla_attention",
    )(q, k, v)


def _even_out_kernel(ctx_ref, x_ref, mod_ref, u_ref, yf_ref, yr_ref, a_ref, d_ref, wglu_ref, bglu_ref,
                     wout_ref, g_ref, h_ref):
    j = pl.program_id(1)
    h = jnp.where(j == 0, ctx_ref[0], x_ref[0])
    m = mod_ref[0, 0]
    u = u_ref[...].astype(F32)
    y = u * d_ref[...] + yf_ref[...].astype(F32) + yr_ref[...].astype(F32)
    act = _gelu_tanh(y)
    s5 = act * jax.nn.sigmoid(_dot(act.astype(BF16), wglu_ref[...]) + bglu_ref[...])
    mix = _dot(s5.astype(BF16), wout_ref[:512]) + _dot(a_ref[0], wout_ref[512:])
    h_ref[0] = h + m[2:3] * _rms(mix, g_ref[...])


def _even_out(ctx, x, modv, u_t, yf_t, yr_t, attn, d_skip, w_glu, b_glu, w_out, g):
    b, s, d = x.shape
    nt = s // TM + 1
    tok = pl.BlockSpec((TM, 512), lambda i, j: (j, i))
    return pl.pallas_call(
        _even_out_kernel,
        out_shape=jax.ShapeDtypeStruct((b, nt * TM, d), F32),
        grid=(b, nt),
        in_specs=[pl.BlockSpec((1, TM, d), lambda i, j: (i, 0, 0)),
                  pl.BlockSpec((1, TM, d), lambda i, j: (i, jnp.maximum(j - 1, 0), 0)),
                  pl.BlockSpec((1, 1, 6, d), lambda i, j: (i, jnp.minimum(j, 1), 0, 0)),
                  tok, tok, tok,
                  pl.BlockSpec((1, TM, 512), lambda i, j: (i, j, 0)),
                  _const_spec(d_skip.shape), _const_spec(w_glu.shape), _const_spec(b_glu.shape),
                  _const_spec(w_out.shape), _const_spec(g.shape)],
        out_specs=pl.BlockSpec((1, TM, d), lambda i, j: (i, j, 0)),
        compiler_params=_params(("arbitrary", "arbitrary"), VMEM_BIG),
        name="even_out",
    )(ctx, x, modv, u_t, yf_t, yr_t, attn, d_skip, w_glu, b_glu, w_out, g)


def _ffn_kernel(h_ref, mod_ref, g2_ref, g3_ref, wg_ref, wu_ref, wd_ref, o_ref):
    h = h_ref[0]
    m = mod_ref[0, 0]
    xn = (_rms(h, g2_ref[...]) * (1.0 + m[4:5]) + m[3:4]).astype(BF16)
    act = (_silu(_dot(xn, wg_ref[...])) * _dot(xn, wu_ref[...])).astype(BF16)
    f = _dot(act, wd_ref[...])
    o_ref[0] = h + m[5:6] * _rms(f, g3_ref[...])


def _ffn(h, modv, g2, g3, wg, wu, wd):
    b, t_all, d = h.shape
    nt = t_all // TM
    one = pl.Buffered(1)
    return pl.pallas_call(
        _ffn_kernel,
        out_shape=jax.ShapeDtypeStruct((b, t_all, d), F32),
        grid=(b, nt),
        in_specs=[pl.BlockSpec((1, TM, d), lambda i, j: (i, j, 0)),
                  pl.BlockSpec((1, 1, 6, d), lambda i, j: (i, jnp.minimum(j, 1), 0, 0)),
                  _const_spec(g2.shape), _const_spec(g3.shape),
                  pl.BlockSpec(wg.shape, lambda i, j: (0, 0), pipeline_mode=one),
                  pl.BlockSpec(wu.shape, lambda i, j: (0, 0), pipeline_mode=one),
                  pl.BlockSpec(wd.shape, lambda i, j: (0, 0), pipeline_mode=one)],
        out_specs=pl.BlockSpec((1, TM, d), lambda i, j: (i, j, 0)),
        compiler_params=_params(("arbitrary", "arbitrary"), VMEM_BIG),
        name="dense_ffn",
    )(h, modv, g2, g3, wg, wu, wd)


def _odd_in_kernel(h_ref, mod_ref, g_ref, win_ref, wg2_ref, bg2_ref, qk_ref, vr_ref, la_ref, *, q_scale):
    m = mod_ref[0, 0]
    xn = (_rms(h_ref[0], g_ref[...]) * (1.0 + m[1:2]) + m[0:1]).astype(BF16)
    z = _dot(xn, win_ref[...])
    qk_ref[0, :, :512] = (z[:, :512] * q_scale).astype(BF16)
    qk_ref[0, :, 512:] = z[:, 512:1024].astype(BF16)
    vr_ref[0] = z[:, 1024:3072].astype(BF16)
    gate = _dot(z[:, 3072:3200].astype(BF16), wg2_ref[...]) + bg2_ref[...]
    la_ref[0] = (jnp.minimum(gate, 0.0) - jnp.log1p(jnp.exp(-jnp.abs(gate)))) * (1.0 / GLA_TAU)


def _odd_in(h, modv, g, w_in, w_gate2, b_gate2, q_scale):
    b, t_all, d = h.shape
    nt = t_all // TM

    def tok(w):
        return pl.BlockSpec((1, TM, w), lambda i, j: (i, j, 0))

    def out(w, dt):
        return jax.ShapeDtypeStruct((b, t_all, w), dt)

    return pl.pallas_call(
        functools.partial(_odd_in_kernel, q_scale=q_scale),
        out_shape=(out(1024, BF16), out(2048, BF16), out(1024, F32)),
        grid=(b, nt),
        in_specs=[tok(d),
                  pl.BlockSpec((1, 1, 6, d), lambda i, j: (i, jnp.minimum(j, 1), 0, 0)),
                  _const_spec(g.shape), _const_spec(w_in.shape), _const_spec(w_gate2.shape),
                  _const_spec(b_gate2.shape)],
        out_specs=(tok(1024), tok(2048), tok(1024)),
        compiler_params=_params(("arbitrary", "arbitrary"), VMEM_BIG),
        name="odd_in",
    )(h, modv, g, w_in, w_gate2, b_gate2)


def _gla_kernel(qf_ref, kf_ref, vf_ref, laf_ref, qb_ref, kb_ref, vb_ref, lab_ref, of_ref, ob_ref,
                sf_ref, sb_ref):
    @pl.when(pl.program_id(1) == 0)
    def _():
        sf_ref[...] = jnp.zeros_like(sf_ref)
        sb_ref[...] = jnp.zeros_like(sb_ref)

    n = qf_ref.shape[1]
    dk = qf_ref.shape[2] // GLA_HEADS
    dv = vf_ref.shape[2] // GLA_HEADS
    row = lax.broadcasted_iota(jnp.int32, (n, n), 0)
    col = lax.broadcasted_iota(jnp.int32, (n, n), 1)
    mid = n // 2
    dirs = ((qf_ref, kf_ref, vf_ref, laf_ref, of_ref, sf_ref, col <= row, n - 1, mid - 1),
            (qb_ref, kb_ref, vb_ref, lab_ref, ob_ref, sb_ref, col >= row, 0, mid))
    for q_ref, k_ref, v_ref, la_ref, o_ref, s_ref, keep, i_tot, i_mid in dirs:
        la = la_ref[0]
        hi = la.astype(BF16)
        lo = (la - hi.astype(F32)).astype(BF16)
        tri = keep.astype(BF16)
        cum = _dot(tri, hi) + _dot(tri, lo)
        tot = cum[i_tot:i_tot + 1]
        cm = cum[i_mid:i_mid + 1]
        qe = q_ref[0].astype(F32) * jnp.exp(cum - cm)
        ke = k_ref[0].astype(F32) * jnp.exp(cm - cum)
        qi = (qe * jnp.exp(cm)).astype(BF16)
        k2 = (ke * jnp.exp(tot - cm)).astype(BF16)
        e_tot = jnp.exp(tot)
        qe = qe.astype(BF16)
        ke = ke.astype(BF16)
        for h in range(GLA_HEADS):
            ks = slice(h * dk, (h + 1) * dk)
            vs = slice(h * dv, (h + 1) * dv)
            v = v_ref[0, :, vs]
            sc = jnp.where(keep, _dot_nt(qe[:, ks], ke[:, ks]), 0.0).astype(BF16)
            st = s_ref[h]
            o_ref[0, :, vs] = (_dot(sc, v) + _dot_nt(qi[:, ks], st.astype(BF16))).astype(o_ref.dtype)
            s_ref[h] = st * e_tot[:, ks] + _dot_tn(v, k2[:, ks])


def _gla_scan(qk, vr, la):
    b, t_all, kw2 = qk.shape
    kw = kw2 // 2
    vw = vr.shape[2] // 2
    n = t_all // TM

    def fwd(col):
        return lambda i, j: (i, j, col)

    def rev(col):
        return lambda i, j: (i, jnp.where(j == 0, 0, n - j), col)

    def spec(w, im):
        return pl.BlockSpec((1, TM, w), im)

    return pl.pallas_call(
        _gla_kernel,
        out_shape=(jax.ShapeDtypeStruct((b, t_all, vw), BF16), jax.ShapeDtypeStruct((b, t_all, vw), BF16)),
        grid=(b, n),
        in_specs=[spec(kw, fwd(0)), spec(kw, fwd(1)), spec(vw, fwd(0)), spec(kw, fwd(0)),
                  spec(kw, rev(0)), spec(kw, rev(1)), spec(vw, rev(0)), spec(kw, rev(1))],
        out_specs=(spec(vw, fwd(0)), spec(vw, rev(0))),
        scratch_shapes=[pltpu.VMEM((GLA_HEADS, vw // GLA_HEADS, kw // GLA_HEADS), F32),
                        pltpu.VMEM((GLA_HEADS, vw // GLA_HEADS, kw // GLA_HEADS), F32)],
        compiler_params=_params(("arbitrary", "arbitrary"), VMEM_BIG),
        name="gla_scan",
    )(qk, qk, vr, la, qk, qk, vr, la)


def _odd_out_kernel(h_ref, mod_ref, of_ref, ob_ref, r_ref, hn_ref, wout_ref, g1_ref, g2_ref, rt_ref,
                    h3_ref, xl_ref, comb_ref):
    m = mod_ref[0, 0]
    o = of_ref[0].astype(F32) + ob_ref[0].astype(F32)
    dv = hn_ref.shape[1]
    parts = [_rms(o[:, h * dv:(h + 1) * dv], hn_ref[...]) for h in range(GLA_HEADS)]
    on = jnp.concatenate(parts, axis=1) * _silu(r_ref[0].astype(F32))
    h3 = h_ref[0] + m[2:3] * _rms(_dot(on.astype(BF16), wout_ref[...]), g1_ref[...])
    h3_ref[0] = h3
    xl = _rms(h3, g2_ref[...]) * (1.0 + m[4:5]) + m[3:4]
    xl_ref[0] = xl.astype(BF16)
    xh = xl.astype(BF16)
    xo = (xl - xh.astype(F32)).astype(BF16)
    rt = rt_ref[...]
    rh = rt.astype(BF16)
    ro = (rt - rh.astype(F32)).astype(BF16)
    logit = _dot_nt(rh, xh) + (_dot_nt(rh, xo) + _dot_nt(ro, xh))
    ne = logit.shape[0]
    eid = lax.broadcasted_iota(jnp.int32, logit.shape, 0)
    m1 = jnp.max(logit, axis=0, keepdims=True)
    i1 = jnp.min(jnp.where(logit == m1, eid, ne), axis=0, keepdims=True)
    rest = jnp.where(eid == i1, -jnp.inf, logit)
    m2 = jnp.max(rest, axis=0, keepdims=True)
    i2 = jnp.min(jnp.where(rest == m2, eid, ne), axis=0, keepdims=True)
    e2 = jnp.exp(m2 - m1)
    w1 = 1.0 / (1.0 + e2)
    comb_ref[...] = jnp.where(eid == i1, w1, 0.0) + jnp.where(eid == i2, e2 * w1, 0.0)


def _odd_out(h, modv, of, ob, r, head_norm, w_out, g1, g2, router_t):
    b, t_all, d = h.shape
    nl = t_all // TM - 1

    def tok(w):
        return pl.BlockSpec((1, TM, w), lambda i, j: (i, j + 1, 0))

    def lat(w):
        return pl.BlockSpec((1, TM, w), lambda i, j: (i, j, 0))

    ne = router_t.shape[0]
    return pl.pallas_call(
        _odd_out_kernel,
        out_shape=(jax.ShapeDtypeStruct((b, nl * TM, d), F32),
                   jax.ShapeDtypeStruct((b, nl * TM, d), BF16),
                   jax.ShapeDtypeStruct((ne, b * nl * TM), F32)),
        grid=(b, nl),
        in_specs=[tok(d),
                  pl.BlockSpec((1, 1, 6, d), lambda i, j: (i, 1, 0, 0)),
                  tok(of.shape[2]), tok(ob.shape[2]),
                  pl.BlockSpec((1, TM, r.shape[2] // 2), lambda i, j: (i, j + 1, 1)),
                  _const_spec(head_norm.shape), _const_spec(w_out.shape), _const_spec(g1.shape),
                  _const_spec(g2.shape), _const_spec(router_t.shape)],
        out_specs=(lat(d), lat(d), pl.BlockSpec((ne, TM), lambda i, j: (0, i * nl + j))),
        compiler_params=_params(("arbitrary", "arbitrary"), VMEM_BIG),
        name="odd_out",
    )(h, modv, of, ob, r, head_norm, w_out, g1, g2, router_t)


def _ceil_count(n, step, cap):
    total = jnp.int32(0)
    for k in range(0, cap, step):
        total = total + (n > k).astype(jnp.int32)
    return total


def _round16(n):
    return lax.shift_left(lax.shift_right_logical(n + 15, 4), 4)


def _moe_kernel(x_ref, ct_ref, wg_ref, wu_ref, wd_ref, y_ref, xs_ref, acc_ref, tri_ref, rank_ref,
                cnt_ref, *, st, ns):
    b = pl.program_id(0)
    e = pl.program_id(1)
    c = pl.program_id(2)
    last_c = pl.num_programs(2) - 1
    ne = ct_ref.shape[0]
    cap = xs_ref.shape[0]
    trip = 2 * MOE_FB

    @pl.when((b == 0) & (e == 0) & (c == 0))
    def _():
        r = lax.broadcasted_iota(jnp.int32, (st, st), 0)
        cc = lax.broadcasted_iota(jnp.int32, (st, st), 1)
        tri_ref[...] = (r < cc).astype(BF16)

    @pl.when((e == 0) & (c == 0))
    def _():
        y_ref[...] = jnp.zeros_like(y_ref)
        xs_ref[...] = jnp.zeros_like(xs_ref)
        acc_ref[...] = jnp.zeros_like(acc_ref)
        for s in range(ns):
            sel = ct_ref[:, s * st:(s + 1) * st] > 0.0
            m16 = jnp.concatenate([sel.astype(F32), jnp.zeros((16 - ne, st), F32)], axis=0).astype(BF16)
            rank = _dot(m16, tri_ref[...])[:ne]
            rank_ref[:, s * st:(s + 1) * st] = jnp.where(sel, rank, -1.0)

    def routed(s):
        cols = slice(s * st, (s + 1) * st)
        return ct_ref[pl.ds(e, 1), cols], rank_ref[pl.ds(e, 1), cols]

    def row_ids(i, rows):
        return (lax.broadcasted_iota(jnp.int32, (rows, st), 0) + i * rows).astype(F32)

    @pl.when(c == 0)
    def _gather():
        off = jnp.int32(0)
        total = jnp.int32(0)
        for s in range(ns):
            w_row, rank = routed(s)
            cnt = jnp.sum((w_row > 0.0).astype(F32)).astype(jnp.int32)
            cnt_ref[s] = cnt

            def gather_block(i, carry, s=s, rank=rank, off=off):
                onehot = jnp.where(rank == row_ids(i, MOE_GB), 1.0, 0.0).astype(BF16)
                rows = _dot(onehot, x_ref[0, s * st:(s + 1) * st, :])
                xs_ref[pl.ds(pl.multiple_of(off + i * MOE_GB, 16), MOE_GB), :] = rows.astype(BF16)
                return carry

            lax.fori_loop(0, _ceil_count(cnt, MOE_GB, st), gather_block, 0)
            total = off + cnt
            off = off + _round16(cnt)
        cnt_ref[ns] = total

        def zero_trip(i, carry):
            acc_ref[pl.ds(pl.multiple_of(i * trip, 16), trip), :] = jnp.zeros((trip, acc_ref.shape[1]), F32)
            return carry

        lax.fori_loop(0, _ceil_count(total, trip, cap), zero_trip, 0)

    def ffn_trip(i, carry):
        for k in range(2):
            rows = pl.ds(pl.multiple_of(i * trip + k * MOE_FB, 16), MOE_FB)
            xb = xs_ref[rows, :]
            act = (_silu(_dot(xb, wg_ref[0])) * _dot(xb, wu_ref[0])).astype(BF16)
            acc_ref[rows, :] += _dot(act, wd_ref[0])
        return carry

    lax.fori_loop(0, _ceil_count(cnt_ref[ns], trip, cap), ffn_trip, 0)

    @pl.when(c == last_c)
    def _scatter():
        off = jnp.int32(0)
        for s in range(ns):
            w_row, rank = routed(s)
            cnt = cnt_ref[s]

            def scatter_block(i, carry, s=s, w_row=w_row, rank=rank, off=off):
                weighted = jnp.where(rank == row_ids(i, MOE_SB), w_row, 0.0).astype(BF16)
                rows = acc_ref[pl.ds(pl.multiple_of(off + i * MOE_SB, 16), MOE_SB), :].astype(BF16)
                y_ref[0, s * st:(s + 1) * st, :] += _dot_tn(weighted, rows)
                return carry

            lax.fori_loop(0, _ceil_count(cnt, MOE_SB, st), scatter_block, 0)
            off = off + _round16(cnt)


def _moe(xl, comb_t, wg, wu, wd):
    b, t, d = xl.shape
    ne, _, f = wg.shape
    st = min(1024, t // 2)
    ns = t // st
    fc = MOE_FC
    trip = 2 * MOE_FB
    need = max(t, (ns - 1) * st + -(-st // MOE_GB) * MOE_GB)
    cap = -(-need // trip) * trip
    return pl.pallas_call(
        functools.partial(_moe_kernel, st=st, ns=ns),
        out_shape=jax.ShapeDtypeStruct((b, t, d), F32),
        grid=(b, ne, f // fc),
        in_specs=[pl.BlockSpec((1, t, d), lambda i, e, c: (i, 0, 0)),
                  pl.BlockSpec((ne, t), lambda i, e, c: (0, i)),
                  pl.BlockSpec((1, d, fc), lambda i, e, c: (e, 0, c)),
                  pl.BlockSpec((1, d, fc), lambda i, e, c: (e, 0, c)),
                  pl.BlockSpec((1, fc, d), lambda i, e, c: (e, c, 0))],
        out_specs=pl.BlockSpec((1, t, d), lambda i, e, c: (i, 0, 0)),
        scratch_shapes=[pltpu.VMEM((cap, d), BF16), pltpu.VMEM((cap, d), F32),
                        pltpu.VMEM((st, st), BF16), pltpu.VMEM((ne, t), F32),
                        pltpu.SMEM((ns + 1,), jnp.int32)],
        compiler_params=_params(("arbitrary", "arbitrary", "arbitrary"), VMEM_BIG),
        name="moe_ffn",
    )(xl, comb_t, wg, wu, wd)


def _final_kernel(h_ref, f_ref, mod_ref, g_ref, o_ref):
    m = mod_ref[0, 0]
    o_ref[0] = h_ref[0] + m[5:6] * _rms(f_ref[0], g_ref[...])


def _final(h3, fl, modv, g):
    b, t, d = h3.shape
    tok = pl.BlockSpec((1, TM, d), lambda i, j: (i, j, 0))
    return pl.pallas_call(
        _final_kernel,
        out_shape=jax.ShapeDtypeStruct((b, t, d), F32),
        grid=(b, t // TM),
        in_specs=[tok, tok, pl.BlockSpec((1, 1, 6, d), lambda i, j: (i, 1, 0, 0)), _const_spec(g.shape)],
        out_specs=tok,
        compiler_params=_params(("arbitrary", "arbitrary")),
        name="final_residual",
    )(h3, fl, modv, g)


def _rope_tables(n_rows, n_ctx):
    half = MLA_ROPE // 2
    inv = 1.0 / (ROPE_BASE ** (jnp.arange(0, half, 2, dtype=F32) / half))
    rows = jnp.repeat(jnp.arange(n_rows, dtype=F32), GRID_W)
    cols = jnp.tile(jnp.arange(GRID_W, dtype=F32), n_rows)
    ang_r = rows[:, None] * inv
    ang_c = cols[:, None] * inv
    cr, sr, cc, sc = jnp.cos(ang_r), jnp.sin(ang_r), jnp.cos(ang_c), jnp.sin(ang_c)
    t = rows.shape[0]
    one = jnp.ones((t, MLA_NOPE), F32)
    zero = jnp.zeros((t, MLA_NOPE), F32)
    z8 = jnp.zeros((t, half // 2), F32)
    pad1 = jnp.ones((t, LANES - MLA_NOPE - MLA_ROPE), F32)
    pad0 = jnp.zeros((t, LANES - MLA_NOPE - MLA_ROPE), F32)
    cos = jnp.concatenate([one, cr, cr, cc, cc, pad1], axis=1)
    s_up = jnp.concatenate([zero, -sr, z8, -sc, z8, pad0], axis=1)
    s_dn = jnp.concatenate([zero, z8, sr, z8, sc, pad0], axis=1)
    tab = jnp.stack([cos, s_up, s_dn])
    ctx = jnp.stack([jnp.ones((n_ctx, LANES), F32), jnp.zeros((n_ctx, LANES), F32),
                     jnp.zeros((n_ctx, LANES), F32)])
    return jnp.concatenate([ctx, tab], axis=1)


def _s5_discretize(lam_re, lam_im, log_dt, b_re, b_im):
    dt = jnp.exp(log_dt)[:, None]
    mag = jnp.exp(lam_re * dt)
    abar_re = mag * jnp.cos(lam_im * dt)
    abar_im = mag * jnp.sin(lam_im * dt)
    den = lam_re * lam_re + lam_im * lam_im
    nr = abar_re - 1.0
    coef_re = (nr * lam_re + abar_im * lam_im) / den
    coef_im = (abar_im * lam_re - nr * lam_im) / den
    bbar_re = coef_re[..., None] * b_re - coef_im[..., None] * b_im
    bbar_im = coef_re[..., None] * b_im + coef_im[..., None] * b_re
    return abar_re, abar_im, bbar_re, bbar_im


def _block_diag(m, per):
    g, r, c = m.shape
    eye = jnp.eye(per, dtype=m.dtype)
    m = m.reshape(g // per, per, r, c)
    return jnp.einsum("sarc,ab->sarbc", m, eye).reshape(g // per, per * r, per * c)


def _s5_layout(lam_re, lam_im, log_dt, b_re, b_im, c_re, c_im):
    per = LANES // S5_GROUP
    a, bbd, cbd = [], [], []
    for d in range(2):
        ar, ai, br, bi = _s5_discretize(lam_re[d], lam_im[d], log_dt[d], b_re[d], b_im[d])
        a.append(jnp.stack([ar.reshape(-1), ai.reshape(-1)]))
        bbd.append(jnp.concatenate([_block_diag(br.transpose(0, 2, 1), per),
                                    _block_diag(bi.transpose(0, 2, 1), per)], axis=2))
        cbd.append(jnp.concatenate([_block_diag(c_re[d].transpose(0, 2, 1), per),
                                    _block_diag(-c_im[d].transpose(0, 2, 1), per)], axis=1))
    return jnp.stack(a), jnp.stack(bbd).astype(BF16), jnp.stack(cbd).astype(BF16)


def _pad_heads(w, heads, width):
    k = w.shape[0]
    w = w.reshape(k, heads, -1)
    return jnp.pad(w, ((0, 0), (0, 0), (0, width - w.shape[2]))).reshape(k, heads * width)


def kernel(x, c, ctx, c_ctx, mod_w, mod_b, norm_g, ev_w_in, s5_lam_re, s5_lam_im, s5_log_dt, s5_b_re, s5_b_im, s5_c_re, s5_c_im, s5_d, s5_w_glu, s5_b_glu, mla_q_norm, mla_w_uq, mla_kv_norm, mla_w_ukv, ev_w_out, ffn_w_gate, ffn_w_up, ffn_w_down, od_w_in, gla_w_gate2, gla_b_gate2, gla_head_norm, od_w_out, moe_router, moe_w_gate, moe_w_up, moe_w_down):
    b, s, d = x.shape
    n_ctx = ctx.shape[1]
    assert n_ctx == TM and s % TM == 0 and mod_w.shape[0] == 2

    rows = -(-(b + 1) // 8) * 8
    cc = jnp.concatenate([c, c_ctx[None], jnp.zeros((rows - b - 1, d), F32)], axis=0)
    mods = _modulation(cc, mod_w, mod_b)

    def modv(i):
        lat = mods[i, :b].reshape(b, 1, 6, d)
        cx = jnp.broadcast_to(mods[i, b].reshape(1, 1, 6, d), (b, 1, 6, d))
        return jnp.concatenate([cx, lat], axis=1)

    def g(i, k):
        return norm_g[i, k].reshape(1, d)

    w_in = ev_w_in[0]
    kr_blk = jnp.pad(w_in[:, 896:928], ((0, 0), (MLA_NOPE, LANES - MLA_NOPE - MLA_ROPE)))
    w_in_p = jnp.concatenate([w_in[:, :896], kr_blk], axis=1).astype(BF16)
    w_uq_p = _pad_heads(mla_w_uq[0], MLA_HEADS, LANES).astype(BF16)
    ukv = mla_w_ukv[0].reshape(-1, MLA_HEADS, MLA_NOPE + MLA_V)
    w_ukv_p = jnp.concatenate(
        [jnp.pad(ukv[:, :, :MLA_NOPE], ((0, 0), (0, 0), (0, LANES - MLA_NOPE))).reshape(ukv.shape[0], -1),
         ukv[:, :, MLA_NOPE:].reshape(ukv.shape[0], -1)], axis=1).astype(BF16)
    tabs = _rope_tables(s // GRID_W, n_ctx)
    m0 = modv(0)
    u_t, q, k, v = _even_in(ctx, x, m0, g(0, 0), w_in_p, mla_q_norm[0].reshape(1, -1), w_uq_p,
                            mla_kv_norm[0].reshape(1, -1), w_ukv_p, tabs,
                            float((MLA_NOPE + MLA_ROPE) ** -0.5 * math.log2(math.e)))
    a_s5, bbd, cbd = _s5_layout(s5_lam_re[0], s5_lam_im[0], s5_log_dt[0], s5_b_re[0], s5_b_im[0],
                                s5_c_re[0], s5_c_im[0])
    t_all = u_t.shape[0]
    yf, yr = _s5_scan(u_t.reshape(t_all * b, 512), a_s5, bbd, cbd, b, n_ctx // S5_TT)
    attn = _attention(q, k, v)
    h1 = _even_out(ctx, x, m0, u_t, yf.reshape(t_all, b * 512), yr.reshape(t_all, b * 512), attn,
                   s5_d[0].reshape(1, -1), s5_w_glu[0].astype(BF16), s5_b_glu[0].reshape(1, -1),
                   ev_w_out[0].astype(BF16), g(0, 1))
    h2 = _ffn(h1, m0, g(0, 2), g(0, 3), ffn_w_gate[0].astype(BF16), ffn_w_up[0].astype(BF16),
              ffn_w_down[0].astype(BF16))

    m1 = modv(1)
    gk = gla_w_gate2.shape[3]
    w_in1 = jnp.pad(od_w_in[0], ((0, 0), (0, LANES - 2 * GLA_GATE_RANK))).astype(BF16)
    wg2 = jnp.zeros((LANES, 2 * gk), F32)
    wg2 = wg2.at[:GLA_GATE_RANK, :gk].set(gla_w_gate2[0, 0])
    wg2 = wg2.at[GLA_GATE_RANK:2 * GLA_GATE_RANK, gk:].set(gla_w_gate2[0, 1]).astype(BF16)
    bg2 = gla_b_gate2[0].reshape(1, 2 * gk)
    qk, vr, la = _odd_in(h2, m1, g(1, 0), w_in1, wg2, bg2, float((gk // GLA_HEADS) ** -0.5))
    of, ob = _gla_scan(qk, vr, la)
    h3, xl, comb_t = _odd_out(h2, m1, of, ob, vr, gla_head_norm[0].reshape(1, -1),
                              od_w_out[0].astype(BF16), g(1, 1), g(1, 2), moe_router[0].T)
    fl = _moe(xl, comb_t, moe_w_gate[0].astype(BF16), moe_w_up[0].astype(BF16),
              moe_w_down[0].astype(BF16))
    return _final(h3, fl, m1, g(1, 3))
```

```python
import functools
import math

import jax
import jax.numpy as jnp
from jax import lax
from jax.experimental import pallas as pl
from jax.experimental.pallas import tpu as pltpu

F32 = jnp.float32
BF16 = jnp.bfloat16

NORM_EPS = 1e-6
GRID_W = 64
S5_GROUP = 16
S5_STATE = 64
MLA_HEADS = 8
MLA_NOPE = 64
MLA_ROPE = 32
MLA_V = 64
ROPE_BASE = 10000.0
GLA_HEADS = 4
GLA_GATE_RANK = 16
GLA_TAU = 16.0
N_EXPERTS = 8

LANES = 128
TM = 256
S5_TT = 32
S5_CB = 512
MOE_GB = 288
MOE_FB = 288
MOE_SB = 256
MOE_FC = 896
VMEM_BIG = 56 * 1024 * 1024


def _dot(a, b):
    return jnp.dot(a, b, preferred_element_type=F32)


def _dot_nt(a, b):
    return lax.dot_general(a, b, (((1,), (1,)), ((), ())), preferred_element_type=F32)


def _dot_tn(a, b):
    return lax.dot_general(a, b, (((0,), (0,)), ((), ())), preferred_element_type=F32)


def _rms(x, g):
    return x * lax.rsqrt(jnp.mean(x * x, axis=-1, keepdims=True) + NORM_EPS) * g


def _silu(x):
    return x * jax.nn.sigmoid(x)


def _gelu_tanh(x):
    return 0.5 * x * (1.0 + jnp.tanh(math.sqrt(2.0 / math.pi) * (x + 0.044715 * (x * x * x))))


def _params(sem, vmem=None):
    return pltpu.CompilerParams(dimension_semantics=sem, vmem_limit_bytes=vmem)


def _const_spec(shape):
    nd = len(shape)
    return pl.BlockSpec(shape, lambda *_: (0,) * nd)


def _mod_kernel(c_ref, w_ref, b_ref, o_ref):
    a = _silu(c_ref[...]).astype(BF16)
    o_ref[0] = _dot(a, w_ref[0].astype(BF16)) + b_ref[0]


def _modulation(cc, mod_w, mod_b):
    depth, d, n = mod_w.shape
    rows = cc.shape[0]
    tn = n // 4
    return pl.pallas_call(
        _mod_kernel,
        out_shape=jax.ShapeDtypeStruct((depth, rows, n), F32),
        grid=(depth, n // tn),
        in_specs=[pl.BlockSpec((rows, d), lambda i, j: (0, 0)),
                  pl.BlockSpec((1, d, tn), lambda i, j: (i, 0, j)),
                  pl.BlockSpec((1, 1, tn), lambda i, j: (i, 0, j))],
        out_specs=pl.BlockSpec((1, rows, tn), lambda i, j: (i, 0, j)),
        compiler_params=_params(("arbitrary", "arbitrary"), VMEM_BIG),
        name="modulation",
    )(cc, mod_w, mod_b.reshape(depth, 1, n))


def _rope(x, tab_ref):
    return x * tab_ref[0] + pltpu.roll(x, LANES - 8, 1) * tab_ref[1] + pltpu.roll(x, 8, 1) * tab_ref[2]


def _even_in_kernel(ctx_ref, x_ref, mod_ref, g_ref, win_ref, qn_ref, wuq_ref, kvn_ref, wukv_ref,
                    tab_ref, u_ref, q_ref, k_ref, v_ref, *, q_scale):
    j = pl.program_id(1)
    xt = jnp.where(j == 0, ctx_ref[0], x_ref[0])
    m = mod_ref[0, 0]
    xn = _rms(xt, g_ref[...]) * (1.0 + m[1:2]) + m[0:1]
    z = _dot(xn.astype(BF16), win_ref[...])
    u_ref[...] = z[:, :512].astype(u_ref.dtype)
    cqn = _rms(z[:, 512:768], qn_ref[...]).astype(BF16)
    qall = _dot(cqn, wuq_ref[...])
    for h in range(MLA_HEADS):
        q_ref[0, h] = (_rope(qall[:, h * LANES:(h + 1) * LANES], tab_ref) * q_scale).astype(BF16)
    ckvn = _rms(z[:, 768:896], kvn_ref[...]).astype(BF16)
    kv = _dot(ckvn, wukv_ref[...])
    kr = _rope(z[:, 896:1024], tab_ref)
    for h in range(MLA_HEADS):
        k_ref[0, h] = (kv[:, h * LANES:(h + 1) * LANES] + kr).astype(BF16)
    v_ref[0] = kv[:, MLA_HEADS * LANES:].astype(BF16)


def _even_in(ctx, x, modv, g, w_in, q_norm, w_uq, kv_norm, w_ukv, tabs, q_scale):
    b, s, d = x.shape
    nt = s // TM + 1
    t_all = nt * TM
    return pl.pallas_call(
        functools.partial(_even_in_kernel, q_scale=q_scale),
        out_shape=(jax.ShapeDtypeStruct((t_all, b * 512), BF16),
                   jax.ShapeDtypeStruct((b, MLA_HEADS, t_all, LANES), BF16),
                   jax.ShapeDtypeStruct((b, MLA_HEADS, t_all, LANES), BF16),
                   jax.ShapeDtypeStruct((b, t_all, MLA_HEADS * MLA_V), BF16)),
        grid=(b, nt),
        in_specs=[pl.BlockSpec((1, TM, d), lambda i, j: (i, 0, 0)),
                  pl.BlockSpec((1, TM, d), lambda i, j: (i, jnp.maximum(j - 1, 0), 0)),
                  pl.BlockSpec((1, 1, 6, d), lambda i, j: (i, jnp.minimum(j, 1), 0, 0)),
                  _const_spec((1, d)),
                  _const_spec(w_in.shape),
                  _const_spec(q_norm.shape),
                  _const_spec(w_uq.shape),
                  _const_spec(kv_norm.shape),
                  _const_spec(w_ukv.shape),
                  pl.BlockSpec((3, TM, LANES), lambda i, j: (0, j, 0))],
        out_specs=(pl.BlockSpec((TM, 512), lambda i, j: (j, i)),
                   pl.BlockSpec((1, MLA_HEADS, TM, LANES), lambda i, j: (i, 0, j, 0)),
                   pl.BlockSpec((1, MLA_HEADS, TM, LANES), lambda i, j: (i, 0, j, 0)),
                   pl.BlockSpec((1, TM, MLA_HEADS * MLA_V), lambda i, j: (i, j, 0))),
        compiler_params=_params(("arbitrary", "arbitrary"), VMEM_BIG),
        name="even_in",
    )(ctx, x, modv, g, w_in, q_norm, w_uq, kv_norm, w_ukv, tabs)


def _s5_scan_kernel(uf_ref, ur_ref, a_ref, bbd_ref, cbd_ref, yf_ref, yr_ref, buf_f, buf_r, st_ref,
                    *, tt, nb):
    @pl.when(pl.program_id(0) == 0)
    def _():
        st_ref[...] = jnp.zeros_like(st_ref)

    nblk = a_ref.shape[-1] // S5_CB
    for d, (u_ref, buf) in enumerate(((uf_ref, buf_f), (ur_ref, buf_r))):
        u = u_ref[...].astype(BF16)
        for c in range(nblk):
            bu = _dot(u[:, c * LANES:(c + 1) * LANES], bbd_ref[d, c])
            buf[0, :, c * S5_CB:(c + 1) * S5_CB] = bu[:, :S5_CB]
            buf[1, :, c * S5_CB:(c + 1) * S5_CB] = bu[:, S5_CB:]

    for c in range(nblk):
        sl = slice(c * S5_CB, (c + 1) * S5_CB)
        afr = jnp.broadcast_to(a_ref[0, 0:1, sl], (nb, S5_CB))
        afi = jnp.broadcast_to(a_ref[0, 1:2, sl], (nb, S5_CB))
        arr = jnp.broadcast_to(a_ref[1, 0:1, sl], (nb, S5_CB))
        ari = jnp.broadcast_to(a_ref[1, 1:2, sl], (nb, S5_CB))

        def body(t, carry, sl=sl, afr=afr, afi=afi, arr=arr, ari=ari):
            fr, fi, rr, ri = carry
            rf = pl.ds(pl.multiple_of(t * nb, nb), nb)
            rb = pl.ds(pl.multiple_of((tt - 1 - t) * nb, nb), nb)
            nfr = afr * fr - afi * fi + buf_f[0, rf, sl]
            nfi = afr * fi + afi * fr + buf_f[1, rf, sl]
            nrr = arr * rr - ari * ri + buf_r[0, rb, sl]
            nri = arr * ri + ari * rr + buf_r[1, rb, sl]
            buf_f[0, rf, sl] = nfr
            buf_f[1, rf, sl] = nfi
            buf_r[0, rb, sl] = nrr
            buf_r[1, rb, sl] = nri
            return nfr, nfi, nrr, nri

        init = (st_ref[0, 0, :, sl], st_ref[0, 1, :, sl], st_ref[1, 0, :, sl], st_ref[1, 1, :, sl])
        fr, fi, rr, ri = lax.fori_loop(0, tt, body, init, unroll=4)
        st_ref[0, 0, :, sl] = fr
        st_ref[0, 1, :, sl] = fi
        st_ref[1, 0, :, sl] = rr
        st_ref[1, 1, :, sl] = ri

    for d, (buf, y_ref) in enumerate(((buf_f, yf_ref), (buf_r, yr_ref))):
        for c in range(nblk):
            sl = slice(c * S5_CB, (c + 1) * S5_CB)
            y_ref[:, c * LANES:(c + 1) * LANES] = (
                _dot(buf[0, :, sl].astype(BF16), cbd_ref[d, c, :S5_CB])
                + _dot(buf[1, :, sl].astype(BF16), cbd_ref[d, c, S5_CB:])).astype(y_ref.dtype)


def _s5_scan(u_t, a, bbd, cbd, nb, n_ctx_tiles):
    rows, w = u_t.shape
    tt = S5_TT
    n = rows // (tt * nb)
    nc = n_ctx_tiles
    width = a.shape[-1]

    def fwd(j):
        return (j, 0)

    def rev(j):
        return (jnp.where(j < nc, nc - 1 - j, n - 1 - j + nc), 0)

    blk = (tt * nb, w)
    return pl.pallas_call(
        functools.partial(_s5_scan_kernel, tt=tt, nb=nb),
        out_shape=(jax.ShapeDtypeStruct((rows, w), BF16), jax.ShapeDtypeStruct((rows, w), BF16)),
        grid=(n,),
        in_specs=[pl.BlockSpec(blk, fwd), pl.BlockSpec(blk, rev),
                  _const_spec(a.shape), _const_spec(bbd.shape), _const_spec(cbd.shape)],
        out_specs=(pl.BlockSpec(blk, fwd), pl.BlockSpec(blk, rev)),
        scratch_shapes=[pltpu.VMEM((2, tt * nb, width), F32),
                        pltpu.VMEM((2, tt * nb, width), F32),
                        pltpu.VMEM((2, 2, nb, width), F32)],
        compiler_params=_params(("arbitrary",), VMEM_BIG),
        name="s5_scan",
    )(u_t, u_t, a, bbd, cbd)


def _attn_kernel(q_ref, k_ref, v_ref, o_ref, *, n_ctx):
    j = pl.program_id(1)
    lane = lax.broadcasted_iota(jnp.int32, (TM, LANES), 1)

    def run(nk):
        for hp in range(MLA_HEADS // 2):
            cols = slice(hp * LANES, (hp + 1) * LANES)
            outs = []
            for h in (2 * hp, 2 * hp + 1):
                s = _dot_nt(q_ref[0, h], k_ref[0, h, :nk])
                p = jnp.exp2(s - jnp.max(s, axis=-1, keepdims=True))
                l = jnp.sum(p, axis=-1, keepdims=True)
                outs.append(_dot(p.astype(BF16), v_ref[0, :nk, cols]) / l)
            o_ref[0, :, cols] = jnp.where(lane < MLA_V, outs[0], outs[1]).astype(o_ref.dtype)

    @pl.when(j == 0)
    def _():
        run(n_ctx)

    @pl.when(j > 0)
    def _():
        run(k_ref.shape[2])


def _attention(q, k, v):
    b, h, t_all, _ = q.shape
    nt = t_all // TM
    return pl.pallas_call(
        functools.partial(_attn_kernel, n_ctx=TM),
        out_shape=jax.ShapeDtypeStruct((b, t_all, h * MLA_V), BF16),
        grid=(b, nt),
        in_specs=[pl.BlockSpec((1, h, TM, LANES), lambda i, j: (i, 0, j, 0)),
                  pl.BlockSpec((1, h, t_all, LANES), lambda i, j: (i, 0, 0, 0)),
                  pl.BlockSpec((1, t_all, h * MLA_V), lambda i, j: (i, 0, 0))],
        out_specs=pl.BlockSpec((1, TM, h * MLA_V), lambda i, j: (i, j, 0)),
        compiler_params=_params(("arbitrary", "arbitrary"), VMEM_BIG),
        name="mla_attention",
    )(q, k, v)


def _even_out_ffn_kernel(ctx_ref, x_ref, mod_ref, u_ref, yf_ref, yr_ref, a_ref, d_ref, wglu_ref, bglu_ref,
                         wout_ref, g1_ref, g2_ref, g3_ref, wg_ref, wu_ref, wd_ref, o_ref):
    j = pl.program_id(1)
    h = jnp.where(j == 0, ctx_ref[0], x_ref[0])
    m = mod_ref[0, 0]
    u = u_ref[...].astype(F32)
    y = u * d_ref[...] + yf_ref[...].astype(F32) + yr_ref[...].astype(F32)
    act = _gelu_tanh(y)
    s5 = act * jax.nn.sigmoid(_dot(act.astype(BF16), wglu_ref[...]) + bglu_ref[...])
    mix = _dot(s5.astype(BF16), wout_ref[:512]) + _dot(a_ref[0], wout_ref[512:])
    h1 = h + m[2:3] * _rms(mix, g1_ref[...])
    xn = (_rms(h1, g2_ref[...]) * (1.0 + m[4:5]) + m[3:4]).astype(BF16)
    hid = (_silu(_dot(xn, wg_ref[...])) * _dot(xn, wu_ref[...])).astype(BF16)
    o_ref[0] = h1 + m[5:6] * _rms(_dot(hid, wd_ref[...]), g3_ref[...])


def _even_out_ffn(ctx, x, modv, u_t, yf_t, yr_t, attn, d_skip, w_glu, b_glu, w_out, g1, g2, g3, wg, wu, wd):
    b, s, d = x.shape
    nt = s // TM + 1
    tok = pl.BlockSpec((TM, 512), lambda i, j: (j, i))
    one = pl.Buffered(1)

    def resident(w):
        return pl.BlockSpec(w.shape, lambda i, j: (0, 0), pipeline_mode=one)

    return pl.pallas_call(
        _even_out_ffn_kernel,
        out_shape=jax.ShapeDtypeStruct((b, nt * TM, d), F32),
        grid=(b, nt),
        in_specs=[pl.BlockSpec((1, TM, d), lambda i, j: (i, 0, 0)),
                  pl.BlockSpec((1, TM, d), lambda i, j: (i, jnp.maximum(j - 1, 0), 0)),
                  pl.BlockSpec((1, 1, 6, d), lambda i, j: (i, jnp.minimum(j, 1), 0, 0)),
                  tok, tok, tok,
                  pl.BlockSpec((1, TM, 512), lambda i, j: (i, j, 0)),
                  _const_spec(d_skip.shape), _const_spec(w_glu.shape), _const_spec(b_glu.shape),
                  resident(w_out), _const_spec(g1.shape), _const_spec(g2.shape), _const_spec(g3.shape),
                  resident(wg), resident(wu), resident(wd)],
        out_specs=pl.BlockSpec((1, TM, d), lambda i, j: (i, j, 0)),
        compiler_params=_params(("arbitrary", "arbitrary"), VMEM_BIG),
        name="even_out_ffn",
    )(ctx, x, modv, u_t, yf_t, yr_t, attn, d_skip, w_glu, b_glu, w_out, g1, g2, g3, wg, wu, wd)


def _odd_in_kernel(h_ref, mod_ref, g_ref, win_ref, wg2_ref, bg2_ref, qk_ref, vr_ref, la_ref, *, q_scale):
    m = mod_ref[0, 0]
    xn = (_rms(h_ref[0], g_ref[...]) * (1.0 + m[1:2]) + m[0:1]).astype(BF16)
    z = _dot(xn, win_ref[...])
    qk_ref[0, :, :512] = (z[:, :512] * q_scale).astype(BF16)
    qk_ref[0, :, 512:] = z[:, 512:1024].astype(BF16)
    vr_ref[0] = z[:, 1024:3072].astype(BF16)
    gate = _dot(z[:, 3072:3200].astype(BF16), wg2_ref[...]) + bg2_ref[...]
    la_ref[0] = (jnp.minimum(gate, 0.0) - jnp.log1p(jnp.exp(-jnp.abs(gate)))) * (1.0 / GLA_TAU)


def _odd_in(h, modv, g, w_in, w_gate2, b_gate2, q_scale):
    b, t_all, d = h.shape
    nt = t_all // TM

    def tok(w):
        return pl.BlockSpec((1, TM, w), lambda i, j: (i, j, 0))

    def out(w, dt):
        return jax.ShapeDtypeStruct((b, t_all, w), dt)

    return pl.pallas_call(
        functools.partial(_odd_in_kernel, q_scale=q_scale),
        out_shape=(out(1024, BF16), out(2048, BF16), out(1024, F32)),
        grid=(b, nt),
        in_specs=[tok(d),
                  pl.BlockSpec((1, 1, 6, d), lambda i, j: (i, jnp.minimum(j, 1), 0, 0)),
                  _const_spec(g.shape), _const_spec(w_in.shape), _const_spec(w_gate2.shape),
                  _const_spec(b_gate2.shape)],
        out_specs=(tok(1024), tok(2048), tok(1024)),
        compiler_params=_params(("arbitrary", "arbitrary"), VMEM_BIG),
        name="odd_in",
    )(h, modv, g, w_in, w_gate2, b_gate2)


def _gla_kernel(qf_ref, kf_ref, vf_ref, laf_ref, qb_ref, kb_ref, vb_ref, lab_ref, of_ref, ob_ref,
                sf_ref, sb_ref):
    @pl.when(pl.program_id(1) == 0)
    def _():
        sf_ref[...] = jnp.zeros_like(sf_ref)
        sb_ref[...] = jnp.zeros_like(sb_ref)

    n = qf_ref.shape[1]
    dk = qf_ref.shape[2] // GLA_HEADS
    dv = vf_ref.shape[2] // GLA_HEADS
    row = lax.broadcasted_iota(jnp.int32, (n, n), 0)
    col = lax.broadcasted_iota(jnp.int32, (n, n), 1)
    mid = n // 2
    dirs = ((qf_ref, kf_ref, vf_ref, laf_ref, of_ref, sf_ref, col <= row, n - 1, mid - 1),
            (qb_ref, kb_ref, vb_ref, lab_ref, ob_ref, sb_ref, col >= row, 0, mid))
    for q_ref, k_ref, v_ref, la_ref, o_ref, s_ref, keep, i_tot, i_mid in dirs:
        la = la_ref[0]
        hi = la.astype(BF16)
        lo = (la - hi.astype(F32)).astype(BF16)
        tri = keep.astype(BF16)
        cum = _dot(tri, hi) + _dot(tri, lo)
        tot = cum[i_tot:i_tot + 1]
        cm = cum[i_mid:i_mid + 1]
        qe = q_ref[0].astype(F32) * jnp.exp(cum - cm)
        ke = k_ref[0].astype(F32) * jnp.exp(cm - cum)
        qi = (qe * jnp.exp(cm)).astype(BF16)
        k2 = (ke * jnp.exp(tot - cm)).astype(BF16)
        e_tot = jnp.exp(tot)
        qe = qe.astype(BF16)
        ke = ke.astype(BF16)
        for h in range(GLA_HEADS):
            ks = slice(h * dk, (h + 1) * dk)
            vs = slice(h * dv, (h + 1) * dv)
            v = v_ref[0, :, vs]
            sc = jnp.where(keep, _dot_nt(qe[:, ks], ke[:, ks]), 0.0).astype(BF16)
            st = s_ref[h]
            o_ref[0, :, vs] = (_dot(sc, v) + _dot_nt(qi[:, ks], st.astype(BF16))).astype(o_ref.dtype)
            s_ref[h] = st * e_tot[:, ks] + _dot_tn(v, k2[:, ks])


def _gla_scan(qk, vr, la):
    b, t_all, kw2 = qk.shape
    kw = kw2 // 2
    vw = vr.shape[2] // 2
    n = t_all // TM

    def fwd(col):
        return lambda i, j: (i, j, col)

    def rev(col):
        return lambda i, j: (i, jnp.where(j == 0, 0, n - j), col)

    def spec(w, im):
        return pl.BlockSpec((1, TM, w), im)

    return pl.pallas_call(
        _gla_kernel,
        out_shape=(jax.ShapeDtypeStruct((b, t_all, vw), BF16), jax.ShapeDtypeStruct((b, t_all, vw), BF16)),
        grid=(b, n),
        in_specs=[spec(kw, fwd(0)), spec(kw, fwd(1)), spec(vw, fwd(0)), spec(kw, fwd(0)),
                  spec(kw, rev(0)), spec(kw, rev(1)), spec(vw, rev(0)), spec(kw, rev(1))],
        out_specs=(spec(vw, fwd(0)), spec(vw, rev(0))),
        scratch_shapes=[pltpu.VMEM((GLA_HEADS, vw // GLA_HEADS, kw // GLA_HEADS), F32),
                        pltpu.VMEM((GLA_HEADS, vw // GLA_HEADS, kw // GLA_HEADS), F32)],
        compiler_params=_params(("arbitrary", "arbitrary"), VMEM_BIG),
        name="gla_scan",
    )(qk, qk, vr, la, qk, qk, vr, la)


def _odd_out_kernel(h_ref, mod_ref, of_ref, ob_ref, r_ref, hn_ref, wout_ref, g1_ref, g2_ref, rt_ref,
                    h3_ref, xl_ref, comb_ref):
    m = mod_ref[0, 0]
    o = of_ref[0].astype(F32) + ob_ref[0].astype(F32)
    dv = hn_ref.shape[1]
    parts = [_rms(o[:, h * dv:(h + 1) * dv], hn_ref[...]) for h in range(GLA_HEADS)]
    on = jnp.concatenate(parts, axis=1) * _silu(r_ref[0].astype(F32))
    h3 = h_ref[0] + m[2:3] * _rms(_dot(on.astype(BF16), wout_ref[...]), g1_ref[...])
    h3_ref[0] = h3
    xl = _rms(h3, g2_ref[...]) * (1.0 + m[4:5]) + m[3:4]
    xl_ref[0] = xl.astype(BF16)
    xh = xl.astype(BF16)
    xo = (xl - xh.astype(F32)).astype(BF16)
    rt = rt_ref[...]
    rh = rt.astype(BF16)
    ro = (rt - rh.astype(F32)).astype(BF16)
    logit = _dot_nt(rh, xh) + (_dot_nt(rh, xo) + _dot_nt(ro, xh))
    ne = logit.shape[0]
    eid = lax.broadcasted_iota(jnp.int32, logit.shape, 0)
    m1 = jnp.max(logit, axis=0, keepdims=True)
    i1 = jnp.min(jnp.where(logit == m1, eid, ne), axis=0, keepdims=True)
    rest = jnp.where(eid == i1, -jnp.inf, logit)
    m2 = jnp.max(rest, axis=0, keepdims=True)
    i2 = jnp.min(jnp.where(rest == m2, eid, ne), axis=0, keepdims=True)
    e2 = jnp.exp(m2 - m1)
    w1 = 1.0 / (1.0 + e2)
    comb_ref[...] = jnp.where(eid == i1, w1, 0.0) + jnp.where(eid == i2, e2 * w1, 0.0)


def _odd_out(h, modv, of, ob, r, head_norm, w_out, g1, g2, router_t):
    b, t_all, d = h.shape
    nl = t_all // TM - 1

    def tok(w):
        return pl.BlockSpec((1, TM, w), lambda i, j: (i, j + 1, 0))

    def lat(w):
        return pl.BlockSpec((1, TM, w), lambda i, j: (i, j, 0))

    ne = router_t.shape[0]
    return pl.pallas_call(
        _odd_out_kernel,
        out_shape=(jax.ShapeDtypeStruct((b, nl * TM, d), F32),
                   jax.ShapeDtypeStruct((b, nl * TM, d), BF16),
                   jax.ShapeDtypeStruct((ne, b * nl * TM), F32)),
        grid=(b, nl),
        in_specs=[tok(d),
                  pl.BlockSpec((1, 1, 6, d), lambda i, j: (i, 1, 0, 0)),
                  tok(of.shape[2]), tok(ob.shape[2]),
                  pl.BlockSpec((1, TM, r.shape[2] // 2), lambda i, j: (i, j + 1, 1)),
                  _const_spec(head_norm.shape), _const_spec(w_out.shape), _const_spec(g1.shape),
                  _const_spec(g2.shape), _const_spec(router_t.shape)],
        out_specs=(lat(d), lat(d), pl.BlockSpec((ne, TM), lambda i, j: (0, i * nl + j))),
        compiler_params=_params(("arbitrary", "arbitrary"), VMEM_BIG),
        name="odd_out",
    )(h, modv, of, ob, r, head_norm, w_out, g1, g2, router_t)


def _ceil_count(n, step, cap):
    total = jnp.int32(0)
    for k in range(0, cap, step):
        total = total + (n > k).astype(jnp.int32)
    return total


def _round16(n):
    return lax.shift_left(lax.shift_right_logical(n + 15, 4), 4)


def _moe_kernel(x_ref, ct_ref, wg_ref, wu_ref, wd_ref, y_ref, xs_ref, acc_ref, tri_ref, rank_ref,
                cnt_ref, *, st, ns):
    b = pl.program_id(0)
    e = pl.program_id(1)
    c = pl.program_id(2)
    last_c = pl.num_programs(2) - 1
    ne = ct_ref.shape[0]
    cap = xs_ref.shape[0]
    trip = 2 * MOE_FB

    @pl.when((b == 0) & (e == 0) & (c == 0))
    def _():
        r = lax.broadcasted_iota(jnp.int32, (st, st), 0)
        cc = lax.broadcasted_iota(jnp.int32, (st, st), 1)
        tri_ref[...] = (r < cc).astype(BF16)

    @pl.when((e == 0) & (c == 0))
    def _():
        y_ref[...] = jnp.zeros_like(y_ref)
        xs_ref[...] = jnp.zeros_like(xs_ref)
        acc_ref[...] = jnp.zeros_like(acc_ref)
        for s in range(ns):
            sel = ct_ref[:, s * st:(s + 1) * st] > 0.0
            m16 = jnp.concatenate([sel.astype(F32), jnp.zeros((16 - ne, st), F32)], axis=0).astype(BF16)
            rank = _dot(m16, tri_ref[...])[:ne]
            rank_ref[:, s * st:(s + 1) * st] = jnp.where(sel, rank, -1.0)

    def routed(s):
        cols = slice(s * st, (s + 1) * st)
        return ct_ref[pl.ds(e, 1), cols], rank_ref[pl.ds(e, 1), cols]

    def row_ids(i, rows):
        return (lax.broadcasted_iota(jnp.int32, (rows, st), 0) + i * rows).astype(F32)

    @pl.when(c == 0)
    def _gather():
        off = jnp.int32(0)
        total = jnp.int32(0)
        for s in range(ns):
            w_row, rank = routed(s)
            cnt = jnp.sum((w_row > 0.0).astype(F32)).astype(jnp.int32)
            cnt_ref[s] = cnt

            def gather_block(i, carry, s=s, rank=rank, off=off):
                onehot = jnp.where(rank == row_ids(i, MOE_GB), 1.0, 0.0).astype(BF16)
                rows = _dot(onehot, x_ref[0, s * st:(s + 1) * st, :])
                xs_ref[pl.ds(pl.multiple_of(off + i * MOE_GB, 16), MOE_GB), :] = rows.astype(BF16)
                return carry

            lax.fori_loop(0, _ceil_count(cnt, MOE_GB, st), gather_block, 0)
            total = off + cnt
            off = off + _round16(cnt)
        cnt_ref[ns] = total

        def zero_trip(i, carry):
            acc_ref[pl.ds(pl.multiple_of(i * trip, 16), trip), :] = jnp.zeros((trip, acc_ref.shape[1]), F32)
            return carry

        lax.fori_loop(0, _ceil_count(total, trip, cap), zero_trip, 0)

    def ffn_trip(i, carry):
        for k in range(2):
            rows = pl.ds(pl.multiple_of(i * trip + k * MOE_FB, 16), MOE_FB)
            xb = xs_ref[rows, :]
            act = (_silu(_dot(xb, wg_ref[0])) * _dot(xb, wu_ref[0])).astype(BF16)
            acc_ref[rows, :] += _dot(act, wd_ref[0])
        return carry

    lax.fori_loop(0, _ceil_count(cnt_ref[ns], trip, cap), ffn_trip, 0)

    @pl.when(c == last_c)
    def _scatter():
        off = jnp.int32(0)
        for s in range(ns):
            w_row, rank = routed(s)
            cnt = cnt_ref[s]

            def scatter_block(i, carry, s=s, w_row=w_row, rank=rank, off=off):
                weighted = jnp.where(rank == row_ids(i, MOE_SB), w_row, 0.0).astype(BF16)
                rows = acc_ref[pl.ds(pl.multiple_of(off + i * MOE_SB, 16), MOE_SB), :].astype(BF16)
                y_ref[0, s * st:(s + 1) * st, :] += _dot_tn(weighted, rows)
                return carry

            lax.fori_loop(0, _ceil_count(cnt, MOE_SB, st), scatter_block, 0)
            off = off + _round16(cnt)


def _moe(xl, comb_t, wg, wu, wd):
    b, t, d = xl.shape
    ne, _, f = wg.shape
    st = min(1024, t // 2)
    ns = t // st
    fc = MOE_FC
    trip = 2 * MOE_FB
    need = max(t, (ns - 1) * st + -(-st // MOE_GB) * MOE_GB)
    cap = -(-need // trip) * trip
    return pl.pallas_call(
        functools.partial(_moe_kernel, st=st, ns=ns),
        out_shape=jax.ShapeDtypeStruct((b, t, d), F32),
        grid=(b, ne, f // fc),
        in_specs=[pl.BlockSpec((1, t, d), lambda i, e, c: (i, 0, 0)),
                  pl.BlockSpec((ne, t), lambda i, e, c: (0, i)),
                  pl.BlockSpec((1, d, fc), lambda i, e, c: (e, 0, c)),
                  pl.BlockSpec((1, d, fc), lambda i, e, c: (e, 0, c)),
                  pl.BlockSpec((1, fc, d), lambda i, e, c: (e, c, 0))],
        out_specs=pl.BlockSpec((1, t, d), lambda i, e, c: (i, 0, 0)),
        scratch_shapes=[pltpu.VMEM((cap, d), BF16), pltpu.VMEM((cap, d), F32),
                        pltpu.VMEM((st, st), BF16), pltpu.VMEM((ne, t), F32),
                        pltpu.SMEM((ns + 1,), jnp.int32)],
        compiler_params=_params(("arbitrary", "arbitrary", "arbitrary"), VMEM_BIG),
        name="moe_ffn",
    )(xl, comb_t, wg, wu, wd)


def _final_kernel(h_ref, f_ref, mod_ref, g_ref, o_ref):
    m = mod_ref[0, 0]
    o_ref[0] = h_ref[0] + m[5:6] * _rms(f_ref[0], g_ref[...])


def _final(h3, fl, modv, g):
    b, t, d = h3.shape
    tok = pl.BlockSpec((1, TM, d), lambda i, j: (i, j, 0))
    return pl.pallas_call(
        _final_kernel,
        out_shape=jax.ShapeDtypeStruct((b, t, d), F32),
        grid=(b, t // TM),
        in_specs=[tok, tok, pl.BlockSpec((1, 1, 6, d), lambda i, j: (i, 1, 0, 0)), _const_spec(g.shape)],
        out_specs=tok,
        compiler_params=_params(("arbitrary", "arbitrary")),
        name="final_residual",
    )(h3, fl, modv, g)


def _rope_tables(n_rows, n_ctx):
    half = MLA_ROPE // 2
    inv = 1.0 / (ROPE_BASE ** (jnp.arange(0, half, 2, dtype=F32) / half))
    rows = jnp.repeat(jnp.arange(n_rows, dtype=F32), GRID_W)
    cols = jnp.tile(jnp.arange(GRID_W, dtype=F32), n_rows)
    ang_r = rows[:, None] * inv
    ang_c = cols[:, None] * inv
    cr, sr, cc, sc = jnp.cos(ang_r), jnp.sin(ang_r), jnp.cos(ang_c), jnp.sin(ang_c)
    t = rows.shape[0]
    one = jnp.ones((t, MLA_NOPE), F32)
    zero = jnp.zeros((t, MLA_NOPE), F32)
    z8 = jnp.zeros((t, half // 2), F32)
    pad1 = jnp.ones((t, LANES - MLA_NOPE - MLA_ROPE), F32)
    pad0 = jnp.zeros((t, LANES - MLA_NOPE - MLA_ROPE), F32)
    cos = jnp.concatenate([one, cr, cr, cc, cc, pad1], axis=1)
    s_up = jnp.concatenate([zero, -sr, z8, -sc, z8, pad0], axis=1)
    s_dn = jnp.concatenate([zero, z8, sr, z8, sc, pad0], axis=1)
    tab = jnp.stack([cos, s_up, s_dn])
    ctx = jnp.stack([jnp.ones((n_ctx, LANES), F32), jnp.zeros((n_ctx, LANES), F32),
                     jnp.zeros((n_ctx, LANES), F32)])
    return jnp.concatenate([ctx, tab], axis=1)


def _s5_discretize(lam_re, lam_im, log_dt, b_re, b_im):
    dt = jnp.exp(log_dt)[:, None]
    mag = jnp.exp(lam_re * dt)
    abar_re = mag * jnp.cos(lam_im * dt)
    abar_im = mag * jnp.sin(lam_im * dt)
    den = lam_re * lam_re + lam_im * lam_im
    nr = abar_re - 1.0
    coef_re = (nr * lam_re + abar_im * lam_im) / den
    coef_im = (abar_im * lam_re - nr * lam_im) / den
    bbar_re = coef_re[..., None] * b_re - coef_im[..., None] * b_im
    bbar_im = coef_re[..., None] * b_im + coef_im[..., None] * b_re
    return abar_re, abar_im, bbar_re, bbar_im


def _block_diag(m, per):
    g, r, c = m.shape
    eye = jnp.eye(per, dtype=m.dtype)
    m = m.reshape(g // per, per, r, c)
    return jnp.einsum("sarc,ab->sarbc", m, eye).reshape(g // per, per * r, per * c)


def _s5_layout(lam_re, lam_im, log_dt, b_re, b_im, c_re, c_im):
    per = LANES // S5_GROUP
    a, bbd, cbd = [], [], []
    for d in range(2):
        ar, ai, br, bi = _s5_discretize(lam_re[d], lam_im[d], log_dt[d], b_re[d], b_im[d])
        a.append(jnp.stack([ar.reshape(-1), ai.reshape(-1)]))
        bbd.append(jnp.concatenate([_block_diag(br.transpose(0, 2, 1), per),
                                    _block_diag(bi.transpose(0, 2, 1), per)], axis=2))
        cbd.append(jnp.concatenate([_block_diag(c_re[d].transpose(0, 2, 1), per),
                                    _block_diag(-c_im[d].transpose(0, 2, 1), per)], axis=1))
    return jnp.stack(a), jnp.stack(bbd).astype(BF16), jnp.stack(cbd).astype(BF16)


def _pad_heads(w, heads, width):
    k = w.shape[0]
    w = w.reshape(k, heads, -1)
    return jnp.pad(w, ((0, 0), (0, 0), (0, width - w.shape[2]))).reshape(k, heads * width)


def kernel(x, c, ctx, c_ctx, mod_w, mod_b, norm_g, ev_w_in, s5_lam_re, s5_lam_im, s5_log_dt, s5_b_re, s5_b_im, s5_c_re, s5_c_im, s5_d, s5_w_glu, s5_b_glu, mla_q_norm, mla_w_uq, mla_kv_norm, mla_w_ukv, ev_w_out, ffn_w_gate, ffn_w_up, ffn_w_down, od_w_in, gla_w_gate2, gla_b_gate2, gla_head_norm, od_w_out, moe_router, moe_w_gate, moe_w_up, moe_w_down):
    b, s, d = x.shape
    n_ctx = ctx.shape[1]
    assert n_ctx == TM and s % TM == 0 and mod_w.shape[0] == 2

    rows = -(-(b + 1) // 8) * 8
    cc = jnp.concatenate([c, c_ctx[None], jnp.zeros((rows - b - 1, d), F32)], axis=0)
    mods = _modulation(cc, mod_w, mod_b)

    def modv(i):
        lat = mods[i, :b].reshape(b, 1, 6, d)
        cx = jnp.broadcast_to(mods[i, b].reshape(1, 1, 6, d), (b, 1, 6, d))
        return jnp.concatenate([cx, lat], axis=1)

    def g(i, k):
        return norm_g[i, k].reshape(1, d)

    w_in = ev_w_in[0]
    kr_blk = jnp.pad(w_in[:, 896:928], ((0, 0), (MLA_NOPE, LANES - MLA_NOPE - MLA_ROPE)))
    w_in_p = jnp.concatenate([w_in[:, :896], kr_blk], axis=1).astype(BF16)
    w_uq_p = _pad_heads(mla_w_uq[0], MLA_HEADS, LANES).astype(BF16)
    ukv = mla_w_ukv[0].reshape(-1, MLA_HEADS, MLA_NOPE + MLA_V)
    w_ukv_p = jnp.concatenate(
        [jnp.pad(ukv[:, :, :MLA_NOPE], ((0, 0), (0, 0), (0, LANES - MLA_NOPE))).reshape(ukv.shape[0], -1),
         ukv[:, :, MLA_NOPE:].reshape(ukv.shape[0], -1)], axis=1).astype(BF16)
    tabs = _rope_tables(s // GRID_W, n_ctx)
    m0 = modv(0)
    u_t, q, k, v = _even_in(ctx, x, m0, g(0, 0), w_in_p, mla_q_norm[0].reshape(1, -1), w_uq_p,
                            mla_kv_norm[0].reshape(1, -1), w_ukv_p, tabs,
                            float((MLA_NOPE + MLA_ROPE) ** -0.5 * math.log2(math.e)))
    a_s5, bbd, cbd = _s5_layout(s5_lam_re[0], s5_lam_im[0], s5_log_dt[0], s5_b_re[0], s5_b_im[0],
                                s5_c_re[0], s5_c_im[0])
    t_all = u_t.shape[0]
    yf, yr = _s5_scan(u_t.reshape(t_all * b, 512), a_s5, bbd, cbd, b, n_ctx // S5_TT)
    attn = _attention(q, k, v)
    h2 = _even_out_ffn(ctx, x, m0, u_t, yf.reshape(t_all, b * 512), yr.reshape(t_all, b * 512), attn,
                       s5_d[0].reshape(1, -1), s5_w_glu[0].astype(BF16), s5_b_glu[0].reshape(1, -1),
                       ev_w_out[0].astype(BF16), g(0, 1), g(0, 2), g(0, 3),
                       ffn_w_gate[0].astype(BF16), ffn_w_up[0].astype(BF16), ffn_w_down[0].astype(BF16))

    m1 = modv(1)
    gk = gla_w_gate2.shape[3]
    w_in1 = jnp.pad(od_w_in[0], ((0, 0), (0, LANES - 2 * GLA_GATE_RANK))).astype(BF16)
    wg2 = jnp.zeros((LANES, 2 * gk), F32)
    wg2 = wg2.at[:GLA_GATE_RANK, :gk].set(gla_w_gate2[0, 0])
    wg2 = wg2.at[GLA_GATE_RANK:2 * GLA_GATE_RANK, gk:].set(gla_w_gate2[0, 1]).astype(BF16)
    bg2 = gla_b_gate2[0].reshape(1, 2 * gk)
    qk, vr, la = _odd_in(h2, m1, g(1, 0), w_in1, wg2, bg2, float((gk // GLA_HEADS) ** -0.5))
    of, ob = _gla_scan(qk, vr, la)
    h3, xl, comb_t = _odd_out(h2, m1, of, ob, vr, gla_head_norm[0].reshape(1, -1),
                              od_w_out[0].astype(BF16), g(1, 1), g(1, 2), moe_router[0].T)
    fl = _moe(xl, comb_t, moe_w_gate[0].astype(BF16), moe_w_up[0].astype(BF16),
              moe_w_down[0].astype(BF16))
    return _final(h3, fl, m1, g(1, 3))
```

```python
import functools
import math

import jax
import jax.numpy as jnp
from jax import lax
from jax.experimental import pallas as pl
from jax.experimental.pallas import tpu as pltpu

F32 = jnp.float32
BF16 = jnp.bfloat16

NORM_EPS = 1e-6
GRID_W = 64
S5_GROUP = 16
S5_STATE = 64
MLA_HEADS = 8
MLA_NOPE = 64
MLA_ROPE = 32
MLA_V = 64
ROPE_BASE = 10000.0
GLA_HEADS = 4
GLA_GATE_RANK = 16
GLA_TAU = 16.0
N_EXPERTS = 8

LANES = 128
TM = 256
S5_TT = 32
S5_CB = 512
MOE_GB = 288
MOE_FB = 288
MOE_SB = 256
MOE_FC = 896
VMEM_BIG = 56 * 1024 * 1024


def _dot(a, b):
    return jnp.dot(a, b, preferred_element_type=F32)


def _dot_nt(a, b):
    return lax.dot_general(a, b, (((1,), (1,)), ((), ())), preferred_element_type=F32)


def _dot_tn(a, b):
    return lax.dot_general(a, b, (((0,), (0,)), ((), ())), preferred_element_type=F32)


def _rms(x, g):
    return x * lax.rsqrt(jnp.mean(x * x, axis=-1, keepdims=True) + NORM_EPS) * g


def _silu(x):
    return x * jax.nn.sigmoid(x)


def _gelu_tanh(x):
    return 0.5 * x * (1.0 + jnp.tanh(math.sqrt(2.0 / math.pi) * (x + 0.044715 * (x * x * x))))


def _params(sem, vmem=None):
    return pltpu.CompilerParams(dimension_semantics=sem, vmem_limit_bytes=vmem)


def _const_spec(shape):
    nd = len(shape)
    return pl.BlockSpec(shape, lambda *_: (0,) * nd)


def _mod_kernel(c_ref, w_ref, b_ref, o_ref):
    a = _silu(c_ref[...]).astype(BF16)
    o_ref[0] = _dot(a, w_ref[0].astype(BF16)) + b_ref[0]


def _modulation(cc, mod_w, mod_b):
    depth, d, n = mod_w.shape
    rows = cc.shape[0]
    tn = n // 4
    return pl.pallas_call(
        _mod_kernel,
        out_shape=jax.ShapeDtypeStruct((depth, rows, n), F32),
        grid=(depth, n // tn),
        in_specs=[pl.BlockSpec((rows, d), lambda i, j: (0, 0)),
                  pl.BlockSpec((1, d, tn), lambda i, j: (i, 0, j)),
                  pl.BlockSpec((1, 1, tn), lambda i, j: (i, 0, j))],
        out_specs=pl.BlockSpec((1, rows, tn), lambda i, j: (i, 0, j)),
        compiler_params=_params(("arbitrary", "arbitrary"), VMEM_BIG),
        name="modulation",
    )(cc, mod_w, mod_b.reshape(depth, 1, n))


def _rope(x, tab_ref):
    return x * tab_ref[0] + pltpu.roll(x, LANES - 8, 1) * tab_ref[1] + pltpu.roll(x, 8, 1) * tab_ref[2]


def _even_in_kernel(ctx_ref, x_ref, mod_ref, g_ref, win_ref, qn_ref, wuq_ref, kvn_ref, wukv_ref,
                    tab_ref, u_ref, q_ref, k_ref, v_ref, *, q_scale):
    j = pl.program_id(1)
    xt = jnp.where(j == 0, ctx_ref[0], x_ref[0])
    m = mod_ref[0, 0]
    xn = _rms(xt, g_ref[...]) * (1.0 + m[1:2]) + m[0:1]
    z = _dot(xn.astype(BF16), win_ref[...])
    u_ref[...] = z[:, :512].astype(u_ref.dtype)
    cqn = _rms(z[:, 512:768], qn_ref[...]).astype(BF16)
    qall = _dot(cqn, wuq_ref[...])
    for h in range(MLA_HEADS):
        q_ref[0, h] = (_rope(qall[:, h * LANES:(h + 1) * LANES], tab_ref) * q_scale).astype(BF16)
    ckvn = _rms(z[:, 768:896], kvn_ref[...]).astype(BF16)
    kv = _dot(ckvn, wukv_ref[...])
    kr = _rope(z[:, 896:1024], tab_ref)
    for h in range(MLA_HEADS):
        k_ref[0, h] = (kv[:, h * LANES:(h + 1) * LANES] + kr).astype(BF16)
    v_ref[0] = kv[:, MLA_HEADS * LANES:].astype(BF16)


def _even_in(ctx, x, modv, g, w_in, q_norm, w_uq, kv_norm, w_ukv, tabs, q_scale):
    b, s, d = x.shape
    nt = s // TM + 1
    t_all = nt * TM
    return pl.pallas_call(
        functools.partial(_even_in_kernel, q_scale=q_scale),
        out_shape=(jax.ShapeDtypeStruct((t_all, b * 512), BF16),
                   jax.ShapeDtypeStruct((b, MLA_HEADS, t_all, LANES), BF16),
                   jax.ShapeDtypeStruct((b, MLA_HEADS, t_all, LANES), BF16),
                   jax.ShapeDtypeStruct((b, t_all, MLA_HEADS * MLA_V), BF16)),
        grid=(b, nt),
        in_specs=[pl.BlockSpec((1, TM, d), lambda i, j: (i, 0, 0)),
                  pl.BlockSpec((1, TM, d), lambda i, j: (i, jnp.maximum(j - 1, 0), 0)),
                  pl.BlockSpec((1, 1, 6, d), lambda i, j: (i, jnp.minimum(j, 1), 0, 0)),
                  _const_spec((1, d)),
                  _const_spec(w_in.shape),
                  _const_spec(q_norm.shape),
                  _const_spec(w_uq.shape),
                  _const_spec(kv_norm.shape),
                  _const_spec(w_ukv.shape),
                  pl.BlockSpec((3, TM, LANES), lambda i, j: (0, j, 0))],
        out_specs=(pl.BlockSpec((TM, 512), lambda i, j: (j, i)),
                   pl.BlockSpec((1, MLA_HEADS, TM, LANES), lambda i, j: (i, 0, j, 0)),
                   pl.BlockSpec((1, MLA_HEADS, TM, LANES), lambda i, j: (i, 0, j, 0)),
                   pl.BlockSpec((1, TM, MLA_HEADS * MLA_V), lambda i, j: (i, j, 0))),
        compiler_params=_params(("arbitrary", "arbitrary"), VMEM_BIG),
        name="even_in",
    )(ctx, x, modv, g, w_in, q_norm, w_uq, kv_norm, w_ukv, tabs)


def _s5_scan_kernel(uf_ref, ur_ref, a_ref, bbd_ref, cbd_ref, yf_ref, yr_ref, buf_f, buf_r, st_ref,
                    *, tt, nb):
    @pl.when(pl.program_id(0) == 0)
    def _():
        st_ref[...] = jnp.zeros_like(st_ref)

    nblk = a_ref.shape[-1] // S5_CB
    dirs = ((uf_ref, buf_f, yf_ref), (ur_ref, buf_r, yr_ref))
    us = [u_ref[...].astype(BF16) for u_ref, _, _ in dirs]
    for c in range(nblk):
        sl = slice(c * S5_CB, (c + 1) * S5_CB)
        for d, (_, buf, _) in enumerate(dirs):
            bu = _dot(us[d][:, c * LANES:(c + 1) * LANES], bbd_ref[d, c])
            buf[0, :, sl] = bu[:, :S5_CB]
            buf[1, :, sl] = bu[:, S5_CB:]

        afr = jnp.broadcast_to(a_ref[0, 0:1, sl], (nb, S5_CB))
        afi = jnp.broadcast_to(a_ref[0, 1:2, sl], (nb, S5_CB))
        arr = jnp.broadcast_to(a_ref[1, 0:1, sl], (nb, S5_CB))
        ari = jnp.broadcast_to(a_ref[1, 1:2, sl], (nb, S5_CB))
        fr, fi, rr, ri = (st_ref[0, 0, :, sl], st_ref[0, 1, :, sl], st_ref[1, 0, :, sl], st_ref[1, 1, :, sl])
        for t in range(tt):
            rf = slice(t * nb, (t + 1) * nb)
            rb = slice((tt - 1 - t) * nb, (tt - t) * nb)
            fr, fi = (afr * fr - afi * fi + buf_f[0, rf, sl], afr * fi + afi * fr + buf_f[1, rf, sl])
            rr, ri = (arr * rr - ari * ri + buf_r[0, rb, sl], arr * ri + ari * rr + buf_r[1, rb, sl])
            buf_f[0, rf, sl] = fr
            buf_f[1, rf, sl] = fi
            buf_r[0, rb, sl] = rr
            buf_r[1, rb, sl] = ri
        st_ref[0, 0, :, sl] = fr
        st_ref[0, 1, :, sl] = fi
        st_ref[1, 0, :, sl] = rr
        st_ref[1, 1, :, sl] = ri

        for d, (_, buf, y_ref) in enumerate(dirs):
            y_ref[:, c * LANES:(c + 1) * LANES] = (
                _dot(buf[0, :, sl].astype(BF16), cbd_ref[d, c, :S5_CB])
                + _dot(buf[1, :, sl].astype(BF16), cbd_ref[d, c, S5_CB:])).astype(y_ref.dtype)


def _s5_scan(u_t, a, bbd, cbd, nb, n_ctx_tiles):
    rows, w = u_t.shape
    tt = S5_TT
    n = rows // (tt * nb)
    nc = n_ctx_tiles
    width = a.shape[-1]

    def fwd(j):
        return (j, 0)

    def rev(j):
        return (jnp.where(j < nc, nc - 1 - j, n - 1 - j + nc), 0)

    blk = (tt * nb, w)
    return pl.pallas_call(
        functools.partial(_s5_scan_kernel, tt=tt, nb=nb),
        out_shape=(jax.ShapeDtypeStruct((rows, w), BF16), jax.ShapeDtypeStruct((rows, w), BF16)),
        grid=(n,),
        in_specs=[pl.BlockSpec(blk, fwd), pl.BlockSpec(blk, rev),
                  _const_spec(a.shape), _const_spec(bbd.shape), _const_spec(cbd.shape)],
        out_specs=(pl.BlockSpec(blk, fwd), pl.BlockSpec(blk, rev)),
        scratch_shapes=[pltpu.VMEM((2, tt * nb, width), F32),
                        pltpu.VMEM((2, tt * nb, width), F32),
                        pltpu.VMEM((2, 2, nb, width), F32)],
        compiler_params=_params(("arbitrary",), VMEM_BIG),
        name="s5_scan",
    )(u_t, u_t, a, bbd, cbd)


def _attn_kernel(q_ref, k_ref, v_ref, o_ref, *, n_ctx):
    j = pl.program_id(1)
    lane = lax.broadcasted_iota(jnp.int32, (TM, LANES), 1)

    def run(nk):
        for hp in range(MLA_HEADS // 2):
            cols = slice(hp * LANES, (hp + 1) * LANES)
            outs = []
            for h in (2 * hp, 2 * hp + 1):
                s = _dot_nt(q_ref[0, h], k_ref[0, h, :nk])
                p = jnp.exp2(s - jnp.max(s, axis=-1, keepdims=True))
                l = jnp.sum(p, axis=-1, keepdims=True)
                outs.append(_dot(p.astype(BF16), v_ref[0, :nk, cols]) / l)
            o_ref[0, :, cols] = jnp.where(lane < MLA_V, outs[0], outs[1]).astype(o_ref.dtype)

    @pl.when(j == 0)
    def _():
        run(n_ctx)

    @pl.when(j > 0)
    def _():
        run(k_ref.shape[2])


def _attention(q, k, v):
    b, h, t_all, _ = q.shape
    nt = t_all // TM
    return pl.pallas_call(
        functools.partial(_attn_kernel, n_ctx=TM),
        out_shape=jax.ShapeDtypeStruct((b, t_all, h * MLA_V), BF16),
        grid=(b, nt),
        in_specs=[pl.BlockSpec((1, h, TM, LANES), lambda i, j: (i, 0, j, 0)),
                  pl.BlockSpec((1, h, t_all, LANES), lambda i, j: (i, 0, 0, 0)),
                  pl.BlockSpec((1, t_all, h * MLA_V), lambda i, j: (i, 0, 0))],
        out_specs=pl.BlockSpec((1, TM, h * MLA_V), lambda i, j: (i, j, 0)),
        compiler_params=_params(("arbitrary", "arbitrary"), VMEM_BIG),
        name="mla_attention",
    )(q, k, v)


def _even_out_ffn_kernel(ctx_ref, x_ref, mod_ref, u_ref, yf_ref, yr_ref, a_ref, d_ref, wglu_ref, bglu_ref,
                         wout_ref, g1_ref, g2_ref, g3_ref, wg_ref, wu_ref, wd_ref, o_ref):
    j = pl.program_id(1)
    h = jnp.where(j == 0, ctx_ref[0], x_ref[0])
    m = mod_ref[0, 0]
    u = u_ref[...].astype(F32)
    y = u * d_ref[...] + yf_ref[...].astype(F32) + yr_ref[...].astype(F32)
    act = _gelu_tanh(y)
    s5 = act * jax.nn.sigmoid(_dot(act.astype(BF16), wglu_ref[...]) + bglu_ref[...])
    mix = _dot(s5.astype(BF16), wout_ref[:512]) + _dot(a_ref[0], wout_ref[512:])
    h1 = h + m[2:3] * _rms(mix, g1_ref[...])
    xn = (_rms(h1, g2_ref[...]) * (1.0 + m[4:5]) + m[3:4]).astype(BF16)
    hid = (_silu(_dot(xn, wg_ref[...])) * _dot(xn, wu_ref[...])).astype(BF16)
    o_ref[0] = h1 + m[5:6] * _rms(_dot(hid, wd_ref[...]), g3_ref[...])


def _even_out_ffn(ctx, x, modv, u_t, yf_t, yr_t, attn, d_skip, w_glu, b_glu, w_out, g1, g2, g3, wg, wu, wd):
    b, s, d = x.shape
    nt = s // TM + 1
    tok = pl.BlockSpec((TM, 512), lambda i, j: (j, i))
    one = pl.Buffered(1)

    def resident(w):
        return pl.BlockSpec(w.shape, lambda i, j: (0, 0), pipeline_mode=one)

    return pl.pallas_call(
        _even_out_ffn_kernel,
        out_shape=jax.ShapeDtypeStruct((b, nt * TM, d), F32),
        grid=(b, nt),
        in_specs=[pl.BlockSpec((1, TM, d), lambda i, j: (i, 0, 0)),
                  pl.BlockSpec((1, TM, d), lambda i, j: (i, jnp.maximum(j - 1, 0), 0)),
                  pl.BlockSpec((1, 1, 6, d), lambda i, j: (i, jnp.minimum(j, 1), 0, 0)),
                  tok, tok, tok,
                  pl.BlockSpec((1, TM, 512), lambda i, j: (i, j, 0)),
                  _const_spec(d_skip.shape), _const_spec(w_glu.shape), _const_spec(b_glu.shape),
                  resident(w_out), _const_spec(g1.shape), _const_spec(g2.shape), _const_spec(g3.shape),
                  resident(wg), resident(wu), resident(wd)],
        out_specs=pl.BlockSpec((1, TM, d), lambda i, j: (i, j, 0)),
        compiler_params=_params(("arbitrary", "arbitrary"), VMEM_BIG),
        name="even_out_ffn",
    )(ctx, x, modv, u_t, yf_t, yr_t, attn, d_skip, w_glu, b_glu, w_out, g1, g2, g3, wg, wu, wd)


def _odd_in_kernel(h_ref, mod_ref, g_ref, win_ref, wg2_ref, bg2_ref, qk_ref, vr_ref, la_ref, *, q_scale):
    m = mod_ref[0, 0]
    xn = (_rms(h_ref[0], g_ref[...]) * (1.0 + m[1:2]) + m[0:1]).astype(BF16)
    z = _dot(xn, win_ref[...])
    qk_ref[0, :, :512] = (z[:, :512] * q_scale).astype(BF16)
    qk_ref[0, :, 512:] = z[:, 512:1024].astype(BF16)
    vr_ref[0] = z[:, 1024:3072].astype(BF16)
    gate = _dot(z[:, 3072:3200].astype(BF16), wg2_ref[...]) + bg2_ref[...]
    la_ref[0] = (jnp.minimum(gate, 0.0) - jnp.log1p(jnp.exp(-jnp.abs(gate)))) * (1.0 / GLA_TAU)


def _odd_in(h, modv, g, w_in, w_gate2, b_gate2, q_scale):
    b, t_all, d = h.shape
    nt = t_all // TM

    def tok(w):
        return pl.BlockSpec((1, TM, w), lambda i, j: (i, j, 0))

    def out(w, dt):
        return jax.ShapeDtypeStruct((b, t_all, w), dt)

    return pl.pallas_call(
        functools.partial(_odd_in_kernel, q_scale=q_scale),
        out_shape=(out(1024, BF16), out(2048, BF16), out(1024, F32)),
        grid=(b, nt),
        in_specs=[tok(d),
                  pl.BlockSpec((1, 1, 6, d), lambda i, j: (i, jnp.minimum(j, 1), 0, 0)),
                  _const_spec(g.shape), _const_spec(w_in.shape), _const_spec(w_gate2.shape),
                  _const_spec(b_gate2.shape)],
        out_specs=(tok(1024), tok(2048), tok(1024)),
        compiler_params=_params(("arbitrary", "arbitrary"), VMEM_BIG),
        name="odd_in",
    )(h, modv, g, w_in, w_gate2, b_gate2)


def _gla_kernel(qf_ref, kf_ref, vf_ref, laf_ref, qb_ref, kb_ref, vb_ref, lab_ref, of_ref, ob_ref,
                sf_ref, sb_ref):
    @pl.when(pl.program_id(1) == 0)
    def _():
        sf_ref[...] = jnp.zeros_like(sf_ref)
        sb_ref[...] = jnp.zeros_like(sb_ref)

    n = qf_ref.shape[1]
    dk = qf_ref.shape[2] // GLA_HEADS
    dv = vf_ref.shape[2] // GLA_HEADS
    row = lax.broadcasted_iota(jnp.int32, (n, n), 0)
    col = lax.broadcasted_iota(jnp.int32, (n, n), 1)
    mid = n // 2
    dirs = ((qf_ref, kf_ref, vf_ref, laf_ref, of_ref, sf_ref, col <= row, n - 1, mid - 1),
            (qb_ref, kb_ref, vb_ref, lab_ref, ob_ref, sb_ref, col >= row, 0, mid))
    for q_ref, k_ref, v_ref, la_ref, o_ref, s_ref, keep, i_tot, i_mid in dirs:
        la = la_ref[0]
        hi = la.astype(BF16)
        lo = (la - hi.astype(F32)).astype(BF16)
        tri = keep.astype(BF16)
        cum = _dot(tri, hi) + _dot(tri, lo)
        tot = cum[i_tot:i_tot + 1]
        cm = cum[i_mid:i_mid + 1]
        qe = q_ref[0].astype(F32) * jnp.exp(cum - cm)
        ke = k_ref[0].astype(F32) * jnp.exp(cm - cum)
        qi = (qe * jnp.exp(cm)).astype(BF16)
        k2 = (ke * jnp.exp(tot - cm)).astype(BF16)
        e_tot = jnp.exp(tot)
        qe = qe.astype(BF16)
        ke = ke.astype(BF16)
        for h in range(GLA_HEADS):
            ks = slice(h * dk, (h + 1) * dk)
            vs = slice(h * dv, (h + 1) * dv)
            v = v_ref[0, :, vs]
            sc = jnp.where(keep, _dot_nt(qe[:, ks], ke[:, ks]), 0.0).astype(BF16)
            st = s_ref[h]
            o_ref[0, :, vs] = (_dot(sc, v) + _dot_nt(qi[:, ks], st.astype(BF16))).astype(o_ref.dtype)
            s_ref[h] = st * e_tot[:, ks] + _dot_tn(v, k2[:, ks])


def _gla_scan(qk, vr, la):
    b, t_all, kw2 = qk.shape
    kw = kw2 // 2
    vw = vr.shape[2] // 2
    n = t_all // TM

    def fwd(col):
        return lambda i, j: (i, j, col)

    def rev(col):
        return lambda i, j: (i, jnp.where(j == 0, 0, n - j), col)

    def spec(w, im):
        return pl.BlockSpec((1, TM, w), im)

    return pl.pallas_call(
        _gla_kernel,
        out_shape=(jax.ShapeDtypeStruct((b, t_all, vw), BF16), jax.ShapeDtypeStruct((b, t_all, vw), BF16)),
        grid=(b, n),
        in_specs=[spec(kw, fwd(0)), spec(kw, fwd(1)), spec(vw, fwd(0)), spec(kw, fwd(0)),
                  spec(kw, rev(0)), spec(kw, rev(1)), spec(vw, rev(0)), spec(kw, rev(1))],
        out_specs=(spec(vw, fwd(0)), spec(vw, rev(0))),
        scratch_shapes=[pltpu.VMEM((GLA_HEADS, vw // GLA_HEADS, kw // GLA_HEADS), F32),
                        pltpu.VMEM((GLA_HEADS, vw // GLA_HEADS, kw // GLA_HEADS), F32)],
        compiler_params=_params(("arbitrary", "arbitrary"), VMEM_BIG),
        name="gla_scan",
    )(qk, qk, vr, la, qk, qk, vr, la)


def _odd_out_kernel(h_ref, mod_ref, of_ref, ob_ref, r_ref, hn_ref, wout_ref, g1_ref, g2_ref, rt_ref,
                    h3_ref, xl_ref, comb_ref):
    m = mod_ref[0, 0]
    o = of_ref[0].astype(F32) + ob_ref[0].astype(F32)
    dv = hn_ref.shape[1]
    parts = [_rms(o[:, h * dv:(h + 1) * dv], hn_ref[...]) for h in range(GLA_HEADS)]
    on = jnp.concatenate(parts, axis=1) * _silu(r_ref[0].astype(F32))
    h3 = h_ref[0] + m[2:3] * _rms(_dot(on.astype(BF16), wout_ref[...]), g1_ref[...])
    h3_ref[0] = h3
    xl = _rms(h3, g2_ref[...]) * (1.0 + m[4:5]) + m[3:4]
    xl_ref[0] = xl.astype(BF16)
    xh = xl.astype(BF16)
    xo = (xl - xh.astype(F32)).astype(BF16)
    rt = rt_ref[...]
    rh = rt.astype(BF16)
    ro = (rt - rh.astype(F32)).astype(BF16)
    logit = _dot_nt(rh, xh) + (_dot_nt(rh, xo) + _dot_nt(ro, xh))
    ne = logit.shape[0]
    eid = lax.broadcasted_iota(jnp.int32, logit.shape, 0)
    m1 = jnp.max(logit, axis=0, keepdims=True)
    i1 = jnp.min(jnp.where(logit == m1, eid, ne), axis=0, keepdims=True)
    rest = jnp.where(eid == i1, -jnp.inf, logit)
    m2 = jnp.max(rest, axis=0, keepdims=True)
    i2 = jnp.min(jnp.where(rest == m2, eid, ne), axis=0, keepdims=True)
    e2 = jnp.exp(m2 - m1)
    w1 = 1.0 / (1.0 + e2)
    comb_ref[...] = jnp.where(eid == i1, w1, 0.0) + jnp.where(eid == i2, e2 * w1, 0.0)


def _odd_out(h, modv, of, ob, r, head_norm, w_out, g1, g2, router_t):
    b, t_all, d = h.shape
    nl = t_all // TM - 1

    def tok(w):
        return pl.BlockSpec((1, TM, w), lambda i, j: (i, j + 1, 0))

    def lat(w):
        return pl.BlockSpec((1, TM, w), lambda i, j: (i, j, 0))

    ne = router_t.shape[0]
    return pl.pallas_call(
        _odd_out_kernel,
        out_shape=(jax.ShapeDtypeStruct((b, nl * TM, d), F32),
                   jax.ShapeDtypeStruct((b, nl * TM, d), BF16),
                   jax.ShapeDtypeStruct((ne, b * nl * TM), F32)),
        grid=(b, nl),
        in_specs=[tok(d),
                  pl.BlockSpec((1, 1, 6, d), lambda i, j: (i, 1, 0, 0)),
                  tok(of.shape[2]), tok(ob.shape[2]),
                  pl.BlockSpec((1, TM, r.shape[2] // 2), lambda i, j: (i, j + 1, 1)),
                  _const_spec(head_norm.shape), _const_spec(w_out.shape), _const_spec(g1.shape),
                  _const_spec(g2.shape), _const_spec(router_t.shape)],
        out_specs=(lat(d), lat(d), pl.BlockSpec((ne, TM), lambda i, j: (0, i * nl + j))),
        compiler_params=_params(("arbitrary", "arbitrary"), VMEM_BIG),
        name="odd_out",
    )(h, modv, of, ob, r, head_norm, w_out, g1, g2, router_t)


def _ceil_count(n, step, cap):
    total = jnp.int32(0)
    for k in range(0, cap, step):
        total = total + (n > k).astype(jnp.int32)
    return total


def _round16(n):
    return lax.shift_left(lax.shift_right_logical(n + 15, 4), 4)


def _moe_kernel(x_ref, ct_ref, wg_ref, wu_ref, wd_ref, y_ref, xs_ref, acc_ref, tri_ref, rank_ref,
                cnt_ref, *, st, ns):
    b = pl.program_id(0)
    e = pl.program_id(1)
    c = pl.program_id(2)
    last_c = pl.num_programs(2) - 1
    ne = ct_ref.shape[0]
    cap = xs_ref.shape[0]
    trip = 2 * MOE_FB

    @pl.when((b == 0) & (e == 0) & (c == 0))
    def _():
        r = lax.broadcasted_iota(jnp.int32, (st, st), 0)
        cc = lax.broadcasted_iota(jnp.int32, (st, st), 1)
        tri_ref[...] = (r < cc).astype(BF16)

    @pl.when((e == 0) & (c == 0))
    def _():
        y_ref[...] = jnp.zeros_like(y_ref)
        xs_ref[...] = jnp.zeros_like(xs_ref)
        acc_ref[...] = jnp.zeros_like(acc_ref)
        for s in range(ns):
            sel = ct_ref[:, s * st:(s + 1) * st] > 0.0
            m16 = jnp.concatenate([sel.astype(F32), jnp.zeros((16 - ne, st), F32)], axis=0).astype(BF16)
            rank = _dot(m16, tri_ref[...])[:ne]
            rank_ref[:, s * st:(s + 1) * st] = jnp.where(sel, rank, -1.0)

    def routed(s):
        cols = slice(s * st, (s + 1) * st)
        return ct_ref[pl.ds(e, 1), cols], rank_ref[pl.ds(e, 1), cols]

    def row_ids(i, rows):
        return (lax.broadcasted_iota(jnp.int32, (rows, st), 0) + i * rows).astype(F32)

    @pl.when(c == 0)
    def _gather():
        off = jnp.int32(0)
        total = jnp.int32(0)
        for s in range(ns):
            w_row, rank = routed(s)
            cnt = jnp.sum((w_row > 0.0).astype(F32)).astype(jnp.int32)
            cnt_ref[s] = cnt

            def gather_block(i, carry, s=s, rank=rank, off=off):
                onehot = jnp.where(rank == row_ids(i, MOE_GB), 1.0, 0.0).astype(BF16)
                rows = _dot(onehot, x_ref[0, s * st:(s + 1) * st, :])
                xs_ref[pl.ds(pl.multiple_of(off + i * MOE_GB, 16), MOE_GB), :] = rows.astype(BF16)
                return carry

            lax.fori_loop(0, _ceil_count(cnt, MOE_GB, st), gather_block, 0)
            total = off + cnt
            off = off + _round16(cnt)
        cnt_ref[ns] = total

        def zero_trip(i, carry):
            acc_ref[pl.ds(pl.multiple_of(i * trip, 16), trip), :] = jnp.zeros((trip, acc_ref.shape[1]), F32)
            return carry

        lax.fori_loop(0, _ceil_count(total, trip, cap), zero_trip, 0)

    def ffn_trip(i, carry):
        for k in range(2):
            rows = pl.ds(pl.multiple_of(i * trip + k * MOE_FB, 16), MOE_FB)
            xb = xs_ref[rows, :]
            act = (_silu(_dot(xb, wg_ref[0])) * _dot(xb, wu_ref[0])).astype(BF16)
            acc_ref[rows, :] += _dot(act, wd_ref[0])
        return carry

    lax.fori_loop(0, _ceil_count(cnt_ref[ns], trip, cap), ffn_trip, 0)

    @pl.when(c == last_c)
    def _scatter():
        off = jnp.int32(0)
        for s in range(ns):
            w_row, rank = routed(s)
            cnt = cnt_ref[s]

            def scatter_block(i, carry, s=s, w_row=w_row, rank=rank, off=off):
                weighted = jnp.where(rank == row_ids(i, MOE_SB), w_row, 0.0).astype(BF16)
                rows = acc_ref[pl.ds(pl.multiple_of(off + i * MOE_SB, 16), MOE_SB), :].astype(BF16)
                y_ref[0, s * st:(s + 1) * st, :] += _dot_tn(weighted, rows)
                return carry

            lax.fori_loop(0, _ceil_count(cnt, MOE_SB, st), scatter_block, 0)
            off = off + _round16(cnt)


def _moe(xl, comb_t, wg, wu, wd):
    b, t, d = xl.shape
    ne, _, f = wg.shape
    st = min(1024, t // 2)
    ns = t // st
    fc = MOE_FC
    trip = 2 * MOE_FB
    need = max(t, (ns - 1) * st + -(-st // MOE_GB) * MOE_GB)
    cap = -(-need // trip) * trip
    return pl.pallas_call(
        functools.partial(_moe_kernel, st=st, ns=ns),
        out_shape=jax.ShapeDtypeStruct((b, t, d), F32),
        grid=(b, ne, f // fc),
        in_specs=[pl.BlockSpec((1, t, d), lambda i, e, c: (i, 0, 0)),
                  pl.BlockSpec((ne, t), lambda i, e, c: (0, i)),
                  pl.BlockSpec((1, d, fc), lambda i, e, c: (e, 0, c)),
                  pl.BlockSpec((1, d, fc), lambda i, e, c: (e, 0, c)),
                  pl.BlockSpec((1, fc, d), lambda i, e, c: (e, c, 0))],
        out_specs=pl.BlockSpec((1, t, d), lambda i, e, c: (i, 0, 0)),
        scratch_shapes=[pltpu.VMEM((cap, d), BF16), pltpu.VMEM((cap, d), F32),
                        pltpu.VMEM((st, st), BF16), pltpu.VMEM((ne, t), F32),
                        pltpu.SMEM((ns + 1,), jnp.int32)],
        compiler_params=_params(("arbitrary", "arbitrary", "arbitrary"), VMEM_BIG),
        name="moe_ffn",
    )(xl, comb_t, wg, wu, wd)


def _final_kernel(h_ref, f_ref, mod_ref, g_ref, o_ref):
    m = mod_ref[0, 0]
    o_ref[0] = h_ref[0] + m[5:6] * _rms(f_ref[0], g_ref[...])


def _final(h3, fl, modv, g):
    b, t, d = h3.shape
    tok = pl.BlockSpec((1, TM, d), lambda i, j: (i, j, 0))
    return pl.pallas_call(
        _final_kernel,
        out_shape=jax.ShapeDtypeStruct((b, t, d), F32),
        grid=(b, t // TM),
        in_specs=[tok, tok, pl.BlockSpec((1, 1, 6, d), lambda i, j: (i, 1, 0, 0)), _const_spec(g.shape)],
        out_specs=tok,
        compiler_params=_params(("arbitrary", "arbitrary")),
        name="final_residual",
    )(h3, fl, modv, g)


def _rope_tables(n_rows, n_ctx):
    half = MLA_ROPE // 2
    inv = 1.0 / (ROPE_BASE ** (jnp.arange(0, half, 2, dtype=F32) / half))
    rows = jnp.repeat(jnp.arange(n_rows, dtype=F32), GRID_W)
    cols = jnp.tile(jnp.arange(GRID_W, dtype=F32), n_rows)
    ang_r = rows[:, None] * inv
    ang_c = cols[:, None] * inv
    cr, sr, cc, sc = jnp.cos(ang_r), jnp.sin(ang_r), jnp.cos(ang_c), jnp.sin(ang_c)
    t = rows.shape[0]
    one = jnp.ones((t, MLA_NOPE), F32)
    zero = jnp.zeros((t, MLA_NOPE), F32)
    z8 = jnp.zeros((t, half // 2), F32)
    pad1 = jnp.ones((t, LANES - MLA_NOPE - MLA_ROPE), F32)
    pad0 = jnp.zeros((t, LANES - MLA_NOPE - MLA_ROPE), F32)
    cos = jnp.concatenate([one, cr, cr, cc, cc, pad1], axis=1)
    s_up = jnp.concatenate([zero, -sr, z8, -sc, z8, pad0], axis=1)
    s_dn = jnp.concatenate([zero, z8, sr, z8, sc, pad0], axis=1)
    tab = jnp.stack([cos, s_up, s_dn])
    ctx = jnp.stack([jnp.ones((n_ctx, LANES), F32), jnp.zeros((n_ctx, LANES), F32),
                     jnp.zeros((n_ctx, LANES), F32)])
    return jnp.concatenate([ctx, tab], axis=1)


def _s5_discretize(lam_re, lam_im, log_dt, b_re, b_im):
    dt = jnp.exp(log_dt)[:, None]
    mag = jnp.exp(lam_re * dt)
    abar_re = mag * jnp.cos(lam_im * dt)
    abar_im = mag * jnp.sin(lam_im * dt)
    den = lam_re * lam_re + lam_im * lam_im
    nr = abar_re - 1.0
    coef_re = (nr * lam_re + abar_im * lam_im) / den
    coef_im = (abar_im * lam_re - nr * lam_im) / den
    bbar_re = coef_re[..., None] * b_re - coef_im[..., None] * b_im
    bbar_im = coef_re[..., None] * b_im + coef_im[..., None] * b_re
    return abar_re, abar_im, bbar_re, bbar_im


def _block_diag(m, per):
    g, r, c = m.shape
    eye = jnp.eye(per, dtype=m.dtype)
    m = m.reshape(g // per, per, r, c)
    return jnp.einsum("sarc,ab->sarbc", m, eye).reshape(g // per, per * r, per * c)


def _s5_layout(lam_re, lam_im, log_dt, b_re, b_im, c_re, c_im):
    per = LANES // S5_GROUP
    a, bbd, cbd = [], [], []
    for d in range(2):
        ar, ai, br, bi = _s5_discretize(lam_re[d], lam_im[d], log_dt[d], b_re[d], b_im[d])
        a.append(jnp.stack([ar.reshape(-1), ai.reshape(-1)]))
        bbd.append(jnp.concatenate([_block_diag(br.transpose(0, 2, 1), per),
                                    _block_diag(bi.transpose(0, 2, 1), per)], axis=2))
        cbd.append(jnp.concatenate([_block_diag(c_re[d].transpose(0, 2, 1), per),
                                    _block_diag(-c_im[d].transpose(0, 2, 1), per)], axis=1))
    return jnp.stack(a), jnp.stack(bbd).astype(BF16), jnp.stack(cbd).astype(BF16)


def _pad_heads(w, heads, width):
    k = w.shape[0]
    w = w.reshape(k, heads, -1)
    return jnp.pad(w, ((0, 0), (0, 0), (0, width - w.shape[2]))).reshape(k, heads * width)


def kernel(x, c, ctx, c_ctx, mod_w, mod_b, norm_g, ev_w_in, s5_lam_re, s5_lam_im, s5_log_dt, s5_b_re, s5_b_im, s5_c_re, s5_c_im, s5_d, s5_w_glu, s5_b_glu, mla_q_norm, mla_w_uq, mla_kv_norm, mla_w_ukv, ev_w_out, ffn_w_gate, ffn_w_up, ffn_w_down, od_w_in, gla_w_gate2, gla_b_gate2, gla_head_norm, od_w_out, moe_router, moe_w_gate, moe_w_up, moe_w_down):
    b, s, d = x.shape
    n_ctx = ctx.shape[1]
    assert n_ctx == TM and s % TM == 0 and mod_w.shape[0] == 2

    rows = -(-(b + 1) // 8) * 8
    cc = jnp.concatenate([c, c_ctx[None], jnp.zeros((rows - b - 1, d), F32)], axis=0)
    mods = _modulation(cc, mod_w, mod_b)

    def modv(i):
        lat = mods[i, :b].reshape(b, 1, 6, d)
        cx = jnp.broadcast_to(mods[i, b].reshape(1, 1, 6, d), (b, 1, 6, d))
        return jnp.concatenate([cx, lat], axis=1)

    def g(i, k):
        return norm_g[i, k].reshape(1, d)

    w_in = ev_w_in[0]
    kr_blk = jnp.pad(w_in[:, 896:928], ((0, 0), (MLA_NOPE, LANES - MLA_NOPE - MLA_ROPE)))
    w_in_p = jnp.concatenate([w_in[:, :896], kr_blk], axis=1).astype(BF16)
    w_uq_p = _pad_heads(mla_w_uq[0], MLA_HEADS, LANES).astype(BF16)
    ukv = mla_w_ukv[0].reshape(-1, MLA_HEADS, MLA_NOPE + MLA_V)
    w_ukv_p = jnp.concatenate(
        [jnp.pad(ukv[:, :, :MLA_NOPE], ((0, 0), (0, 0), (0, LANES - MLA_NOPE))).reshape(ukv.shape[0], -1),
         ukv[:, :, MLA_NOPE:].reshape(ukv.shape[0], -1)], axis=1).astype(BF16)
    tabs = _rope_tables(s // GRID_W, n_ctx)
    m0 = modv(0)
    u_t, q, k, v = _even_in(ctx, x, m0, g(0, 0), w_in_p, mla_q_norm[0].reshape(1, -1), w_uq_p,
                            mla_kv_norm[0].reshape(1, -1), w_ukv_p, tabs,
                            float((MLA_NOPE + MLA_ROPE) ** -0.5 * math.log2(math.e)))
    a_s5, bbd, cbd = _s5_layout(s5_lam_re[0], s5_lam_im[0], s5_log_dt[0], s5_b_re[0], s5_b_im[0],
                                s5_c_re[0], s5_c_im[0])
    t_all = u_t.shape[0]
    yf, yr = _s5_scan(u_t.reshape(t_all * b, 512), a_s5, bbd, cbd, b, n_ctx // S5_TT)
    attn = _attention(q, k, v)
    h2 = _even_out_ffn(ctx, x, m0, u_t, yf.reshape(t_all, b * 512), yr.reshape(t_all, b * 512), attn,
                       s5_d[0].reshape(1, -1), s5_w_glu[0].astype(BF16), s5_b_glu[0].reshape(1, -1),
                       ev_w_out[0].astype(BF16), g(0, 1), g(0, 2), g(0, 3),
                       ffn_w_gate[0].astype(BF16), ffn_w_up[0].astype(BF16), ffn_w_down[0].astype(BF16))

    m1 = modv(1)
    gk = gla_w_gate2.shape[3]
    w_in1 = jnp.pad(od_w_in[0], ((0, 0), (0, LANES - 2 * GLA_GATE_RANK))).astype(BF16)
    wg2 = jnp.zeros((LANES, 2 * gk), F32)
    wg2 = wg2.at[:GLA_GATE_RANK, :gk].set(gla_w_gate2[0, 0])
    wg2 = wg2.at[GLA_GATE_RANK:2 * GLA_GATE_RANK, gk:].set(gla_w_gate2[0, 1]).astype(BF16)
    bg2 = gla_b_gate2[0].reshape(1, 2 * gk)
    qk, vr, la = _odd_in(h2, m1, g(1, 0), w_in1, wg2, bg2, float((gk // GLA_HEADS) ** -0.5))
    of, ob = _gla_scan(qk, vr, la)
    h3, xl, comb_t = _odd_out(h2, m1, of, ob, vr, gla_head_norm[0].reshape(1, -1),
                              od_w_out[0].astype(BF16), g(1, 1), g(1, 2), moe_router[0].T)
    fl = _moe(xl, comb_t, moe_w_gate[0].astype(BF16), moe_w_up[0].astype(BF16),
              moe_w_down[0].astype(BF16))
    return _final(h3, fl, m1, g(1, 3))
```

```python
import functools
import math

import jax
import jax.numpy as jnp
from jax import lax
from jax.experimental import pallas as pl
from jax.experimental.pallas import tpu as pltpu

F32 = jnp.float32
BF16 = jnp.bfloat16

NORM_EPS = 1e-6
GRID_W = 64
S5_GROUP = 16
S5_STATE = 64
MLA_HEADS = 8
MLA_NOPE = 64
MLA_ROPE = 32
MLA_V = 64
ROPE_BASE = 10000.0
GLA_HEADS = 4
GLA_GATE_RANK = 16
GLA_TAU = 16.0
N_EXPERTS = 8

LANES = 128
TM = 256
S5_TT = 64
S5_CB = 512
MOE_GB = 288
MOE_FB = 288
MOE_SB = 256
MOE_FC = 896
VMEM_BIG = 56 * 1024 * 1024


def _dot(a, b):
    return jnp.dot(a, b, preferred_element_type=F32)


def _dot_nt(a, b):
    return lax.dot_general(a, b, (((1,), (1,)), ((), ())), preferred_element_type=F32)


def _dot_tn(a, b):
    return lax.dot_general(a, b, (((0,), (0,)), ((), ())), preferred_element_type=F32)


def _rms(x, g):
    return x * lax.rsqrt(jnp.mean(x * x, axis=-1, keepdims=True) + NORM_EPS) * g


def _silu(x):
    return x * jax.nn.sigmoid(x)


def _gelu_tanh(x):
    return 0.5 * x * (1.0 + jnp.tanh(math.sqrt(2.0 / math.pi) * (x + 0.044715 * (x * x * x))))


def _params(sem, vmem=None):
    return pltpu.CompilerParams(dimension_semantics=sem, vmem_limit_bytes=vmem)


def _const_spec(shape):
    nd = len(shape)
    return pl.BlockSpec(shape, lambda *_: (0,) * nd)


def _mod_kernel(c_ref, w_ref, b_ref, o_ref):
    a = _silu(c_ref[...]).astype(BF16)
    o_ref[0] = _dot(a, w_ref[0].astype(BF16)) + b_ref[0]


def _modulation(cc, mod_w, mod_b):
    depth, d, n = mod_w.shape
    rows = cc.shape[0]
    tn = n // 4
    return pl.pallas_call(
        _mod_kernel,
        out_shape=jax.ShapeDtypeStruct((depth, rows, n), F32),
        grid=(depth, n // tn),
        in_specs=[pl.BlockSpec((rows, d), lambda i, j: (0, 0)),
                  pl.BlockSpec((1, d, tn), lambda i, j: (i, 0, j)),
                  pl.BlockSpec((1, 1, tn), lambda i, j: (i, 0, j))],
        out_specs=pl.BlockSpec((1, rows, tn), lambda i, j: (i, 0, j)),
        compiler_params=_params(("arbitrary", "arbitrary"), VMEM_BIG),
        name="modulation",
    )(cc, mod_w, mod_b.reshape(depth, 1, n))


def _rope(x, tab_ref):
    return x * tab_ref[0] + pltpu.roll(x, LANES - 8, 1) * tab_ref[1] + pltpu.roll(x, 8, 1) * tab_ref[2]


def _even_in_kernel(ctx_ref, x_ref, mod_ref, g_ref, win_ref, qn_ref, wuq_ref, kvn_ref, wukv_ref,
                    tab_ref, u_ref, q_ref, k_ref, v_ref, *, q_scale):
    j = pl.program_id(1)
    xt = jnp.where(j == 0, ctx_ref[0], x_ref[0])
    m = mod_ref[0, 0]
    xn = _rms(xt, g_ref[...]) * (1.0 + m[1:2]) + m[0:1]
    z = _dot(xn.astype(BF16), win_ref[...])
    u_ref[...] = z[:, :512].astype(u_ref.dtype)
    cqn = _rms(z[:, 512:768], qn_ref[...]).astype(BF16)
    qall = _dot(cqn, wuq_ref[...])
    for h in range(MLA_HEADS):
        q_ref[0, h] = (_rope(qall[:, h * LANES:(h + 1) * LANES], tab_ref) * q_scale).astype(BF16)
    ckvn = _rms(z[:, 768:896], kvn_ref[...]).astype(BF16)
    kv = _dot(ckvn, wukv_ref[...])
    kr = _rope(z[:, 896:1024], tab_ref)
    for h in range(MLA_HEADS):
        k_ref[0, h] = (kv[:, h * LANES:(h + 1) * LANES] + kr).astype(BF16)
    v_ref[0] = kv[:, MLA_HEADS * LANES:].astype(BF16)


def _even_in(ctx, x, modv, g, w_in, q_norm, w_uq, kv_norm, w_ukv, tabs, q_scale):
    b, s, d = x.shape
    nt = s // TM + 1
    t_all = nt * TM
    return pl.pallas_call(
        functools.partial(_even_in_kernel, q_scale=q_scale),
        out_shape=(jax.ShapeDtypeStruct((t_all, b * 512), BF16),
                   jax.ShapeDtypeStruct((b, MLA_HEADS, t_all, LANES), BF16),
                   jax.ShapeDtypeStruct((b, MLA_HEADS, t_all, LANES), BF16),
                   jax.ShapeDtypeStruct((b, t_all, MLA_HEADS * MLA_V), BF16)),
        grid=(b, nt),
        in_specs=[pl.BlockSpec((1, TM, d), lambda i, j: (i, 0, 0)),
                  pl.BlockSpec((1, TM, d), lambda i, j: (i, jnp.maximum(j - 1, 0), 0)),
                  pl.BlockSpec((1, 1, 6, d), lambda i, j: (i, jnp.minimum(j, 1), 0, 0)),
                  _const_spec((1, d)),
                  _const_spec(w_in.shape),
                  _const_spec(q_norm.shape),
                  _const_spec(w_uq.shape),
                  _const_spec(kv_norm.shape),
                  _const_spec(w_ukv.shape),
                  pl.BlockSpec((3, TM, LANES), lambda i, j: (0, j, 0))],
        out_specs=(pl.BlockSpec((TM, 512), lambda i, j: (j, i)),
                   pl.BlockSpec((1, MLA_HEADS, TM, LANES), lambda i, j: (i, 0, j, 0)),
                   pl.BlockSpec((1, MLA_HEADS, TM, LANES), lambda i, j: (i, 0, j, 0)),
                   pl.BlockSpec((1, TM, MLA_HEADS * MLA_V), lambda i, j: (i, j, 0))),
        compiler_params=_params(("arbitrary", "arbitrary"), VMEM_BIG),
        name="even_in",
    )(ctx, x, modv, g, w_in, q_norm, w_uq, kv_norm, w_ukv, tabs)


def _s5_scan_kernel(uf_ref, ur_ref, a_ref, bbd_ref, cbd_ref, yf_ref, yr_ref, buf_f, buf_r, st_ref,
                    *, tt, nb):
    @pl.when(pl.program_id(0) == 0)
    def _():
        st_ref[...] = jnp.zeros_like(st_ref)

    nblk = a_ref.shape[-1] // S5_CB
    dirs = ((uf_ref, buf_f, yf_ref), (ur_ref, buf_r, yr_ref))
    us = [u_ref[...].astype(BF16) for u_ref, _, _ in dirs]
    for c in range(nblk):
        sl = slice(c * S5_CB, (c + 1) * S5_CB)
        for d, (_, buf, _) in enumerate(dirs):
            bu = _dot(us[d][:, c * LANES:(c + 1) * LANES], bbd_ref[d, c])
            buf[0, :, sl] = bu[:, :S5_CB]
            buf[1, :, sl] = bu[:, S5_CB:]

        afr = jnp.broadcast_to(a_ref[0, 0:1, sl], (nb, S5_CB))
        afi = jnp.broadcast_to(a_ref[0, 1:2, sl], (nb, S5_CB))
        arr = jnp.broadcast_to(a_ref[1, 0:1, sl], (nb, S5_CB))
        ari = jnp.broadcast_to(a_ref[1, 1:2, sl], (nb, S5_CB))
        fr, fi, rr, ri = (st_ref[0, 0, :, sl], st_ref[0, 1, :, sl], st_ref[1, 0, :, sl], st_ref[1, 1, :, sl])
        for t in range(tt):
            rf = slice(t * nb, (t + 1) * nb)
            rb = slice((tt - 1 - t) * nb, (tt - t) * nb)
            fr, fi = (afr * fr - afi * fi + buf_f[0, rf, sl], afr * fi + afi * fr + buf_f[1, rf, sl])
            rr, ri = (arr * rr - ari * ri + buf_r[0, rb, sl], arr * ri + ari * rr + buf_r[1, rb, sl])
            buf_f[0, rf, sl] = fr
            buf_f[1, rf, sl] = fi
            buf_r[0, rb, sl] = rr
            buf_r[1, rb, sl] = ri
        st_ref[0, 0, :, sl] = fr
        st_ref[0, 1, :, sl] = fi
        st_ref[1, 0, :, sl] = rr
        st_ref[1, 1, :, sl] = ri

        for d, (_, buf, y_ref) in enumerate(dirs):
            y_ref[:, c * LANES:(c + 1) * LANES] = (
                _dot(buf[0, :, sl].astype(BF16), cbd_ref[d, c, :S5_CB])
                + _dot(buf[1, :, sl].astype(BF16), cbd_ref[d, c, S5_CB:])).astype(y_ref.dtype)


def _s5_scan(u_t, a, bbd, cbd, nb, n_ctx_tiles):
    rows, w = u_t.shape
    tt = S5_TT
    n = rows // (tt * nb)
    nc = n_ctx_tiles
    width = a.shape[-1]

    def fwd(j):
        return (j, 0)

    def rev(j):
        return (jnp.where(j < nc, nc - 1 - j, n - 1 - j + nc), 0)

    blk = (tt * nb, w)
    return pl.pallas_call(
        functools.partial(_s5_scan_kernel, tt=tt, nb=nb),
        out_shape=(jax.ShapeDtypeStruct((rows, w), BF16), jax.ShapeDtypeStruct((rows, w), BF16)),
        grid=(n,),
        in_specs=[pl.BlockSpec(blk, fwd), pl.BlockSpec(blk, rev),
                  _const_spec(a.shape), _const_spec(bbd.shape), _const_spec(cbd.shape)],
        out_specs=(pl.BlockSpec(blk, fwd), pl.BlockSpec(blk, rev)),
        scratch_shapes=[pltpu.VMEM((2, tt * nb, width), F32),
                        pltpu.VMEM((2, tt * nb, width), F32),
                        pltpu.VMEM((2, 2, nb, width), F32)],
        compiler_params=_params(("arbitrary",), VMEM_BIG),
        name="s5_scan",
    )(u_t, u_t, a, bbd, cbd)


def _attn_kernel(q_ref, k_ref, v_ref, o_ref, *, n_ctx):
    j = pl.program_id(1)
    lane = lax.broadcasted_iota(jnp.int32, (TM, LANES), 1)

    def run(nk):
        for hp in range(MLA_HEADS // 2):
            cols = slice(hp * LANES, (hp + 1) * LANES)
            outs = []
            for h in (2 * hp, 2 * hp + 1):
                s = _dot_nt(q_ref[0, h], k_ref[0, h, :nk])
                p = jnp.exp2(s - jnp.max(s, axis=-1, keepdims=True))
                l = jnp.sum(p, axis=-1, keepdims=True)
                outs.append(_dot(p.astype(BF16), v_ref[0, :nk, cols]) / l)
            o_ref[0, :, cols] = jnp.where(lane < MLA_V, outs[0], outs[1]).astype(o_ref.dtype)

    @pl.when(j == 0)
    def _():
        run(n_ctx)

    @pl.when(j > 0)
    def _():
        run(k_ref.shape[2])


def _attention(q, k, v):
    b, h, t_all, _ = q.shape
    nt = t_all // TM
    return pl.pallas_call(
        functools.partial(_attn_kernel, n_ctx=TM),
        out_shape=jax.ShapeDtypeStruct((b, t_all, h * MLA_V), BF16),
        grid=(b, nt),
        in_specs=[pl.BlockSpec((1, h, TM, LANES), lambda i, j: (i, 0, j, 0)),
                  pl.BlockSpec((1, h, t_all, LANES), lambda i, j: (i, 0, 0, 0)),
                  pl.BlockSpec((1, t_all, h * MLA_V), lambda i, j: (i, 0, 0))],
        out_specs=pl.BlockSpec((1, TM, h * MLA_V), lambda i, j: (i, j, 0)),
        compiler_params=_params(("arbitrary", "arbitrary"), VMEM_BIG),
        name="mla_attention",
    )(q, k, v)


def _even_out_ffn_kernel(ctx_ref, x_ref, mod_ref, u_ref, yf_ref, yr_ref, a_ref, d_ref, wglu_ref, bglu_ref,
                         wout_ref, g1_ref, g2_ref, g3_ref, wg_ref, wu_ref, wd_ref, o_ref):
    j = pl.program_id(1)
    h = jnp.where(j == 0, ctx_ref[0], x_ref[0])
    m = mod_ref[0, 0]
    u = u_ref[...].astype(F32)
    y = u * d_ref[...] + yf_ref[...].astype(F32) + yr_ref[...].astype(F32)
    act = _gelu_tanh(y)
    s5 = act * jax.nn.sigmoid(_dot(act.astype(BF16), wglu_ref[...]) + bglu_ref[...])
    mix = _dot(s5.astype(BF16), wout_ref[:512]) + _dot(a_ref[0], wout_ref[512:])
    h1 = h + m[2:3] * _rms(mix, g1_ref[...])
    xn = (_rms(h1, g2_ref[...]) * (1.0 + m[4:5]) + m[3:4]).astype(BF16)
    hid = (_silu(_dot(xn, wg_ref[...])) * _dot(xn, wu_ref[...])).astype(BF16)
    o_ref[0] = h1 + m[5:6] * _rms(_dot(hid, wd_ref[...]), g3_ref[...])


def _even_out_ffn(ctx, x, modv, u_t, yf_t, yr_t, attn, d_skip, w_glu, b_glu, w_out, g1, g2, g3, wg, wu, wd):
    b, s, d = x.shape
    nt = s // TM + 1
    tok = pl.BlockSpec((TM, 512), lambda i, j: (j, i))
    one = pl.Buffered(1)

    def resident(w):
        return pl.BlockSpec(w.shape, lambda i, j: (0, 0), pipeline_mode=one)

    return pl.pallas_call(
        _even_out_ffn_kernel,
        out_shape=jax.ShapeDtypeStruct((b, nt * TM, d), F32),
        grid=(b, nt),
        in_specs=[pl.BlockSpec((1, TM, d), lambda i, j: (i, 0, 0)),
                  pl.BlockSpec((1, TM, d), lambda i, j: (i, jnp.maximum(j - 1, 0), 0)),
                  pl.BlockSpec((1, 1, 6, d), lambda i, j: (i, jnp.minimum(j, 1), 0, 0)),
                  tok, tok, tok,
                  pl.BlockSpec((1, TM, 512), lambda i, j: (i, j, 0)),
                  _const_spec(d_skip.shape), _const_spec(w_glu.shape), _const_spec(b_glu.shape),
                  resident(w_out), _const_spec(g1.shape), _const_spec(g2.shape), _const_spec(g3.shape),
                  resident(wg), resident(wu), resident(wd)],
        out_specs=pl.BlockSpec((1, TM, d), lambda i, j: (i, j, 0)),
        compiler_params=_params(("arbitrary", "arbitrary"), VMEM_BIG),
        name="even_out_ffn",
    )(ctx, x, modv, u_t, yf_t, yr_t, attn, d_skip, w_glu, b_glu, w_out, g1, g2, g3, wg, wu, wd)


def _odd_in_kernel(h_ref, mod_ref, g_ref, win_ref, wg2_ref, bg2_ref, qk_ref, vr_ref, la_ref, *, q_scale):
    m = mod_ref[0, 0]
    xn = (_rms(h_ref[0], g_ref[...]) * (1.0 + m[1:2]) + m[0:1]).astype(BF16)
    z = _dot(xn, win_ref[...])
    qk_ref[0, :, :512] = (z[:, :512] * q_scale).astype(BF16)
    qk_ref[0, :, 512:] = z[:, 512:1024].astype(BF16)
    vr_ref[0] = z[:, 1024:3072].astype(BF16)
    gate = _dot(z[:, 3072:3200].astype(BF16), wg2_ref[...]) + bg2_ref[...]
    la_ref[0] = (jnp.minimum(gate, 0.0) - jnp.log1p(jnp.exp(-jnp.abs(gate)))) * (1.0 / GLA_TAU)


def _odd_in(h, modv, g, w_in, w_gate2, b_gate2, q_scale):
    b, t_all, d = h.shape
    nt = t_all // TM

    def tok(w):
        return pl.BlockSpec((1, TM, w), lambda i, j: (i, j, 0))

    def out(w, dt):
        return jax.ShapeDtypeStruct((b, t_all, w), dt)

    return pl.pallas_call(
        functools.partial(_odd_in_kernel, q_scale=q_scale),
        out_shape=(out(1024, BF16), out(2048, BF16), out(1024, F32)),
        grid=(b, nt),
        in_specs=[tok(d),
                  pl.BlockSpec((1, 1, 6, d), lambda i, j: (i, jnp.minimum(j, 1), 0, 0)),
                  _const_spec(g.shape), _const_spec(w_in.shape), _const_spec(w_gate2.shape),
                  _const_spec(b_gate2.shape)],
        out_specs=(tok(1024), tok(2048), tok(1024)),
        compiler_params=_params(("arbitrary", "arbitrary"), VMEM_BIG),
        name="odd_in",
    )(h, modv, g, w_in, w_gate2, b_gate2)


def _gla_kernel(qf_ref, kf_ref, vf_ref, laf_ref, qb_ref, kb_ref, vb_ref, lab_ref, of_ref, ob_ref,
                sf_ref, sb_ref):
    @pl.when(pl.program_id(1) == 0)
    def _():
        sf_ref[...] = jnp.zeros_like(sf_ref)
        sb_ref[...] = jnp.zeros_like(sb_ref)

    n = qf_ref.shape[1]
    dk = qf_ref.shape[2] // GLA_HEADS
    dv = vf_ref.shape[2] // GLA_HEADS
    row = lax.broadcasted_iota(jnp.int32, (n, n), 0)
    col = lax.broadcasted_iota(jnp.int32, (n, n), 1)
    mid = n // 2
    dirs = ((qf_ref, kf_ref, vf_ref, laf_ref, of_ref, sf_ref, col <= row, n - 1, mid - 1),
            (qb_ref, kb_ref, vb_ref, lab_ref, ob_ref, sb_ref, col >= row, 0, mid))
    for q_ref, k_ref, v_ref, la_ref, o_ref, s_ref, keep, i_tot, i_mid in dirs:
        la = la_ref[0]
        hi = la.astype(BF16)
        lo = (la - hi.astype(F32)).astype(BF16)
        tri = keep.astype(BF16)
        cum = _dot(tri, hi) + _dot(tri, lo)
        tot = cum[i_tot:i_tot + 1]
        cm = cum[i_mid:i_mid + 1]
        qe = q_ref[0].astype(F32) * jnp.exp(cum - cm)
        ke = k_ref[0].astype(F32) * jnp.exp(cm - cum)
        qi = (qe * jnp.exp(cm)).astype(BF16)
        k2 = (ke * jnp.exp(tot - cm)).astype(BF16)
        e_tot = jnp.exp(tot)
        qe = qe.astype(BF16)
        ke = ke.astype(BF16)
        for h in range(GLA_HEADS):
            ks = slice(h * dk, (h + 1) * dk)
            vs = slice(h * dv, (h + 1) * dv)
            v = v_ref[0, :, vs]
            sc = jnp.where(keep, _dot_nt(qe[:, ks], ke[:, ks]), 0.0).astype(BF16)
            st = s_ref[h]
            o_ref[0, :, vs] = (_dot(sc, v) + _dot_nt(qi[:, ks], st.astype(BF16))).astype(o_ref.dtype)
            s_ref[h] = st * e_tot[:, ks] + _dot_tn(v, k2[:, ks])


def _gla_scan(qk, vr, la):
    b, t_all, kw2 = qk.shape
    kw = kw2 // 2
    vw = vr.shape[2] // 2
    n = t_all // TM

    def fwd(col):
        return lambda i, j: (i, j, col)

    def rev(col):
        return lambda i, j: (i, jnp.where(j == 0, 0, n - j), col)

    def spec(w, im):
        return pl.BlockSpec((1, TM, w), im)

    return pl.pallas_call(
        _gla_kernel,
        out_shape=(jax.ShapeDtypeStruct((b, t_all, vw), BF16), jax.ShapeDtypeStruct((b, t_all, vw), BF16)),
        grid=(b, n),
        in_specs=[spec(kw, fwd(0)), spec(kw, fwd(1)), spec(vw, fwd(0)), spec(kw, fwd(0)),
                  spec(kw, rev(0)), spec(kw, rev(1)), spec(vw, rev(0)), spec(kw, rev(1))],
        out_specs=(spec(vw, fwd(0)), spec(vw, rev(0))),
        scratch_shapes=[pltpu.VMEM((GLA_HEADS, vw // GLA_HEADS, kw // GLA_HEADS), F32),
                        pltpu.VMEM((GLA_HEADS, vw // GLA_HEADS, kw // GLA_HEADS), F32)],
        compiler_params=_params(("arbitrary", "arbitrary"), VMEM_BIG),
        name="gla_scan",
    )(qk, qk, vr, la, qk, qk, vr, la)


def _odd_out_kernel(h_ref, mod_ref, of_ref, ob_ref, r_ref, hn_ref, wout_ref, g1_ref, g2_ref, rt_ref,
                    h3_ref, xl_ref, comb_ref):
    m = mod_ref[0, 0]
    o = of_ref[0].astype(F32) + ob_ref[0].astype(F32)
    dv = hn_ref.shape[1]
    parts = [_rms(o[:, h * dv:(h + 1) * dv], hn_ref[...]) for h in range(GLA_HEADS)]
    on = jnp.concatenate(parts, axis=1) * _silu(r_ref[0].astype(F32))
    h3 = h_ref[0] + m[2:3] * _rms(_dot(on.astype(BF16), wout_ref[...]), g1_ref[...])
    h3_ref[0] = h3
    xl = _rms(h3, g2_ref[...]) * (1.0 + m[4:5]) + m[3:4]
    xl_ref[0] = xl.astype(BF16)
    xh = xl.astype(BF16)
    xo = (xl - xh.astype(F32)).astype(BF16)
    rt = rt_ref[...]
    rh = rt.astype(BF16)
    ro = (rt - rh.astype(F32)).astype(BF16)
    logit = _dot_nt(rh, xh) + (_dot_nt(rh, xo) + _dot_nt(ro, xh))
    ne = logit.shape[0]
    eid = lax.broadcasted_iota(jnp.int32, logit.shape, 0)
    m1 = jnp.max(logit, axis=0, keepdims=True)
    i1 = jnp.min(jnp.where(logit == m1, eid, ne), axis=0, keepdims=True)
    rest = jnp.where(eid == i1, -jnp.inf, logit)
    m2 = jnp.max(rest, axis=0, keepdims=True)
    i2 = jnp.min(jnp.where(rest == m2, eid, ne), axis=0, keepdims=True)
    e2 = jnp.exp(m2 - m1)
    w1 = 1.0 / (1.0 + e2)
    comb_ref[...] = jnp.where(eid == i1, w1, 0.0) + jnp.where(eid == i2, e2 * w1, 0.0)


def _odd_out(h, modv, of, ob, r, head_norm, w_out, g1, g2, router_t):
    b, t_all, d = h.shape
    nl = t_all // TM - 1

    def tok(w):
        return pl.BlockSpec((1, TM, w), lambda i, j: (i, j + 1, 0))

    def lat(w):
        return pl.BlockSpec((1, TM, w), lambda i, j: (i, j, 0))

    ne = router_t.shape[0]
    return pl.pallas_call(
        _odd_out_kernel,
        out_shape=(jax.ShapeDtypeStruct((b, nl * TM, d), F32),
                   jax.ShapeDtypeStruct((b, nl * TM, d), BF16),
                   jax.ShapeDtypeStruct((ne, b * nl * TM), F32)),
        grid=(b, nl),
        in_specs=[tok(d),
                  pl.BlockSpec((1, 1, 6, d), lambda i, j: (i, 1, 0, 0)),
                  tok(of.shape[2]), tok(ob.shape[2]),
                  pl.BlockSpec((1, TM, r.shape[2] // 2), lambda i, j: (i, j + 1, 1)),
                  _const_spec(head_norm.shape), _const_spec(w_out.shape), _const_spec(g1.shape),
                  _const_spec(g2.shape), _const_spec(router_t.shape)],
        out_specs=(lat(d), lat(d), pl.BlockSpec((ne, TM), lambda i, j: (0, i * nl + j))),
        compiler_params=_params(("arbitrary", "arbitrary"), VMEM_BIG),
        name="odd_out",
    )(h, modv, of, ob, r, head_norm, w_out, g1, g2, router_t)


def _ceil_count(n, step, cap):
    total = jnp.int32(0)
    for k in range(0, cap, step):
        total = total + (n > k).astype(jnp.int32)
    return total


def _round16(n):
    return lax.shift_left(lax.shift_right_logical(n + 15, 4), 4)


def _moe_kernel(x_ref, ct_ref, wg_ref, wu_ref, wd_ref, y_ref, xs_ref, acc_ref, tri_ref, rank_ref,
                cnt_ref, *, st, ns):
    b = pl.program_id(0)
    e = pl.program_id(1)
    c = pl.program_id(2)
    last_c = pl.num_programs(2) - 1
    ne = ct_ref.shape[0]
    cap = xs_ref.shape[0]
    trip = 2 * MOE_FB

    @pl.when((b == 0) & (e == 0) & (c == 0))
    def _():
        r = lax.broadcasted_iota(jnp.int32, (st, st), 0)
        cc = lax.broadcasted_iota(jnp.int32, (st, st), 1)
        tri_ref[...] = (r < cc).astype(BF16)

    @pl.when((e == 0) & (c == 0))
    def _():
        y_ref[...] = jnp.zeros_like(y_ref)
        xs_ref[...] = jnp.zeros_like(xs_ref)
        acc_ref[...] = jnp.zeros_like(acc_ref)
        for s in range(ns):
            sel = ct_ref[:, s * st:(s + 1) * st] > 0.0
            m16 = jnp.concatenate([sel.astype(F32), jnp.zeros((16 - ne, st), F32)], axis=0).astype(BF16)
            rank = _dot(m16, tri_ref[...])[:ne]
            rank_ref[:, s * st:(s + 1) * st] = jnp.where(sel, rank, -1.0)

    def routed(s):
        cols = slice(s * st, (s + 1) * st)
        return ct_ref[pl.ds(e, 1), cols], rank_ref[pl.ds(e, 1), cols]

    def row_ids(i, rows):
        return (lax.broadcasted_iota(jnp.int32, (rows, st), 0) + i * rows).astype(F32)

    @pl.when(c == 0)
    def _gather():
        off = jnp.int32(0)
        total = jnp.int32(0)
        for s in range(ns):
            w_row, rank = routed(s)
            cnt = jnp.sum((w_row > 0.0).astype(F32)).astype(jnp.int32)
            cnt_ref[s] = cnt

            def gather_block(i, carry, s=s, rank=rank, off=off):
                onehot = jnp.where(rank == row_ids(i, MOE_GB), 1.0, 0.0).astype(BF16)
                rows = _dot(onehot, x_ref[0, s * st:(s + 1) * st, :])
                xs_ref[pl.ds(pl.multiple_of(off + i * MOE_GB, 16), MOE_GB), :] = rows.astype(BF16)
                return carry

            lax.fori_loop(0, _ceil_count(cnt, MOE_GB, st), gather_block, 0)
            total = off + cnt
            off = off + _round16(cnt)
        cnt_ref[ns] = total

        def zero_trip(i, carry):
            acc_ref[pl.ds(pl.multiple_of(i * trip, 16), trip), :] = jnp.zeros((trip, acc_ref.shape[1]), F32)
            return carry

        lax.fori_loop(0, _ceil_count(total, trip, cap), zero_trip, 0)

    def ffn_trip(i, carry):
        for k in range(2):
            rows = pl.ds(pl.multiple_of(i * trip + k * MOE_FB, 16), MOE_FB)
            xb = xs_ref[rows, :]
            act = (_silu(_dot(xb, wg_ref[0])) * _dot(xb, wu_ref[0])).astype(BF16)
            acc_ref[rows, :] += _dot(act, wd_ref[0])
        return carry

    lax.fori_loop(0, _ceil_count(cnt_ref[ns], trip, cap), ffn_trip, 0)

    @pl.when(c == last_c)
    def _scatter():
        off = jnp.int32(0)
        for s in range(ns):
            w_row, rank = routed(s)
            cnt = cnt_ref[s]

            def scatter_block(i, carry, s=s, w_row=w_row, rank=rank, off=off):
                weighted = jnp.where(rank == row_ids(i, MOE_SB), w_row, 0.0).astype(BF16)
                rows = acc_ref[pl.ds(pl.multiple_of(off + i * MOE_SB, 16), MOE_SB), :].astype(BF16)
                y_ref[0, s * st:(s + 1) * st, :] += _dot_tn(weighted, rows)
                return carry

            lax.fori_loop(0, _ceil_count(cnt, MOE_SB, st), scatter_block, 0)
            off = off + _round16(cnt)


def _moe(xl, comb_t, wg, wu, wd):
    b, t, d = xl.shape
    ne, _, f = wg.shape
    st = min(1024, t // 2)
    ns = t // st
    fc = MOE_FC
    trip = 2 * MOE_FB
    need = max(t, (ns - 1) * st + -(-st // MOE_GB) * MOE_GB)
    cap = -(-need // trip) * trip
    return pl.pallas_call(
        functools.partial(_moe_kernel, st=st, ns=ns),
        out_shape=jax.ShapeDtypeStruct((b, t, d), F32),
        grid=(b, ne, f // fc),
        in_specs=[pl.BlockSpec((1, t, d), lambda i, e, c: (i, 0, 0)),
                  pl.BlockSpec((ne, t), lambda i, e, c: (0, i)),
                  pl.BlockSpec((1, d, fc), lambda i, e, c: (e, 0, c)),
                  pl.BlockSpec((1, d, fc), lambda i, e, c: (e, 0, c)),
                  pl.BlockSpec((1, fc, d), lambda i, e, c: (e, c, 0))],
        out_specs=pl.BlockSpec((1, t, d), lambda i, e, c: (i, 0, 0)),
        scratch_shapes=[pltpu.VMEM((cap, d), BF16), pltpu.VMEM((cap, d), F32),
                        pltpu.VMEM((st, st), BF16), pltpu.VMEM((ne, t), F32),
                        pltpu.SMEM((ns + 1,), jnp.int32)],
        compiler_params=_params(("arbitrary", "arbitrary", "arbitrary"), VMEM_BIG),
        name="moe_ffn",
    )(xl, comb_t, wg, wu, wd)


def _final_kernel(h_ref, f_ref, mod_ref, g_ref, o_ref):
    m = mod_ref[0, 0]
    o_ref[0] = h_ref[0] + m[5:6] * _rms(f_ref[0], g_ref[...])


def _final(h3, fl, modv, g):
    b, t, d = h3.shape
    tok = pl.BlockSpec((1, TM, d), lambda i, j: (i, j, 0))
    return pl.pallas_call(
        _final_kernel,
        out_shape=jax.ShapeDtypeStruct((b, t, d), F32),
        grid=(b, t // TM),
        in_specs=[tok, tok, pl.BlockSpec((1, 1, 6, d), lambda i, j: (i, 1, 0, 0)), _const_spec(g.shape)],
        out_specs=tok,
        compiler_params=_params(("arbitrary", "arbitrary")),
        name="final_residual",
    )(h3, fl, modv, g)


def _rope_tables(n_rows, n_ctx):
    half = MLA_ROPE // 2
    inv = 1.0 / (ROPE_BASE ** (jnp.arange(0, half, 2, dtype=F32) / half))
    rows = jnp.repeat(jnp.arange(n_rows, dtype=F32), GRID_W)
    cols = jnp.tile(jnp.arange(GRID_W, dtype=F32), n_rows)
    ang_r = rows[:, None] * inv
    ang_c = cols[:, None] * inv
    cr, sr, cc, sc = jnp.cos(ang_r), jnp.sin(ang_r), jnp.cos(ang_c), jnp.sin(ang_c)
    t = rows.shape[0]
    one = jnp.ones((t, MLA_NOPE), F32)
    zero = jnp.zeros((t, MLA_NOPE), F32)
    z8 = jnp.zeros((t, half // 2), F32)
    pad1 = jnp.ones((t, LANES - MLA_NOPE - MLA_ROPE), F32)
    pad0 = jnp.zeros((t, LANES - MLA_NOPE - MLA_ROPE), F32)
    cos = jnp.concatenate([one, cr, cr, cc, cc, pad1], axis=1)
    s_up = jnp.concatenate([zero, -sr, z8, -sc, z8, pad0], axis=1)
    s_dn = jnp.concatenate([zero, z8, sr, z8, sc, pad0], axis=1)
    tab = jnp.stack([cos, s_up, s_dn])
    ctx = jnp.stack([jnp.ones((n_ctx, LANES), F32), jnp.zeros((n_ctx, LANES), F32),
                     jnp.zeros((n_ctx, LANES), F32)])
    return jnp.concatenate([ctx, tab], axis=1)


def _s5_discretize(lam_re, lam_im, log_dt, b_re, b_im):
    dt = jnp.exp(log_dt)[:, None]
    mag = jnp.exp(lam_re * dt)
    abar_re = mag * jnp.cos(lam_im * dt)
    abar_im = mag * jnp.sin(lam_im * dt)
    den = lam_re * lam_re + lam_im * lam_im
    nr = abar_re - 1.0
    coef_re = (nr * lam_re + abar_im * lam_im) / den
    coef_im = (abar_im * lam_re - nr * lam_im) / den
    bbar_re = coef_re[..., None] * b_re - coef_im[..., None] * b_im
    bbar_im = coef_re[..., None] * b_im + coef_im[..., None] * b_re
    return abar_re, abar_im, bbar_re, bbar_im


def _block_diag(m, per):
    g, r, c = m.shape
    eye = jnp.eye(per, dtype=m.dtype)
    m = m.reshape(g // per, per, r, c)
    return jnp.einsum("sarc,ab->sarbc", m, eye).reshape(g // per, per * r, per * c)


def _s5_layout(lam_re, lam_im, log_dt, b_re, b_im, c_re, c_im):
    per = LANES // S5_GROUP
    a, bbd, cbd = [], [], []
    for d in range(2):
        ar, ai, br, bi = _s5_discretize(lam_re[d], lam_im[d], log_dt[d], b_re[d], b_im[d])
        a.append(jnp.stack([ar.reshape(-1), ai.reshape(-1)]))
        bbd.append(jnp.concatenate([_block_diag(br.transpose(0, 2, 1), per),
                                    _block_diag(bi.transpose(0, 2, 1), per)], axis=2))
        cbd.append(jnp.concatenate([_block_diag(c_re[d].transpose(0, 2, 1), per),
                                    _block_diag(-c_im[d].transpose(0, 2, 1), per)], axis=1))
    return jnp.stack(a), jnp.stack(bbd).astype(BF16), jnp.stack(cbd).astype(BF16)


def _pad_heads(w, heads, width):
    k = w.shape[0]
    w = w.reshape(k, heads, -1)
    return jnp.pad(w, ((0, 0), (0, 0), (0, width - w.shape[2]))).reshape(k, heads * width)


def kernel(x, c, ctx, c_ctx, mod_w, mod_b, norm_g, ev_w_in, s5_lam_re, s5_lam_im, s5_log_dt, s5_b_re, s5_b_im, s5_c_re, s5_c_im, s5_d, s5_w_glu, s5_b_glu, mla_q_norm, mla_w_uq, mla_kv_norm, mla_w_ukv, ev_w_out, ffn_w_gate, ffn_w_up, ffn_w_down, od_w_in, gla_w_gate2, gla_b_gate2, gla_head_norm, od_w_out, moe_router, moe_w_gate, moe_w_up, moe_w_down):
    b, s, d = x.shape
    n_ctx = ctx.shape[1]
    assert n_ctx == TM and s % TM == 0 and mod_w.shape[0] == 2

    rows = -(-(b + 1) // 8) * 8
    cc = jnp.concatenate([c, c_ctx[None], jnp.zeros((rows - b - 1, d), F32)], axis=0)
    mods = _modulation(cc, mod_w, mod_b)

    def modv(i):
        lat = mods[i, :b].reshape(b, 1, 6, d)
        cx = jnp.broadcast_to(mods[i, b].reshape(1, 1, 6, d), (b, 1, 6, d))
        return jnp.concatenate([cx, lat], axis=1)

    def g(i, k):
        return norm_g[i, k].reshape(1, d)

    w_in = ev_w_in[0]
    kr_blk = jnp.pad(w_in[:, 896:928], ((0, 0), (MLA_NOPE, LANES - MLA_NOPE - MLA_ROPE)))
    w_in_p = jnp.concatenate([w_in[:, :896], kr_blk], axis=1).astype(BF16)
    w_uq_p = _pad_heads(mla_w_uq[0], MLA_HEADS, LANES).astype(BF16)
    ukv = mla_w_ukv[0].reshape(-1, MLA_HEADS, MLA_NOPE + MLA_V)
    w_ukv_p = jnp.concatenate(
        [jnp.pad(ukv[:, :, :MLA_NOPE], ((0, 0), (0, 0), (0, LANES - MLA_NOPE))).reshape(ukv.shape[0], -1),
         ukv[:, :, MLA_NOPE:].reshape(ukv.shape[0], -1)], axis=1).astype(BF16)
    tabs = _rope_tables(s // GRID_W, n_ctx)
    m0 = modv(0)
    u_t, q, k, v = _even_in(ctx, x, m0, g(0, 0), w_in_p, mla_q_norm[0].reshape(1, -1), w_uq_p,
                            mla_kv_norm[0].reshape(1, -1), w_ukv_p, tabs,
                            float((MLA_NOPE + MLA_ROPE) ** -0.5 * math.log2(math.e)))
    a_s5, bbd, cbd = _s5_layout(s5_lam_re[0], s5_lam_im[0], s5_log_dt[0], s5_b_re[0], s5_b_im[0],
                                s5_c_re[0], s5_c_im[0])
    t_all = u_t.shape[0]
    yf, yr = _s5_scan(u_t.reshape(t_all * b, 512), a_s5, bbd, cbd, b, n_ctx // S5_TT)
    attn = _attention(q, k, v)
    h2 = _even_out_ffn(ctx, x, m0, u_t, yf.reshape(t_all, b * 512), yr.reshape(t_all, b * 512), attn,
                       s5_d[0].reshape(1, -1), s5_w_glu[0].astype(BF16), s5_b_glu[0].reshape(1, -1),
                       ev_w_out[0].astype(BF16), g(0, 1), g(0, 2), g(0, 3),
                       ffn_w_gate[0].astype(BF16), ffn_w_up[0].astype(BF16), ffn_w_down[0].astype(BF16))

    m1 = modv(1)
    gk = gla_w_gate2.shape[3]
    w_in1 = jnp.pad(od_w_in[0], ((0, 0), (0, LANES - 2 * GLA_GATE_RANK))).astype(BF16)
    wg2 = jnp.zeros((LANES, 2 * gk), F32)
    wg2 = wg2.at[:GLA_GATE_RANK, :gk].set(gla_w_gate2[0, 0])
    wg2 = wg2.at[GLA_GATE_RANK:2 * GLA_GATE_RANK, gk:].set(gla_w_gate2[0, 1]).astype(BF16)
    bg2 = gla_b_gate2[0].reshape(1, 2 * gk)
    qk, vr, la = _odd_in(h2, m1, g(1, 0), w_in1, wg2, bg2, float((gk // GLA_HEADS) ** -0.5))
    of, ob = _gla_scan(qk, vr, la)
    h3, xl, comb_t = _odd_out(h2, m1, of, ob, vr, gla_head_norm[0].reshape(1, -1),
                              od_w_out[0].astype(BF16), g(1, 1), g(1, 2), moe_router[0].T)
    fl = _moe(xl, comb_t, moe_w_gate[0].astype(BF16), moe_w_up[0].astype(BF16),
              moe_w_down[0].astype(BF16))
    return _final(h3, fl, m1, g(1, 3))
```
